```python
import math
import jax, jax.numpy as jnp
from jax import lax
import numpy as np


D_MODEL = 2048
BATCH = 1
SEQ = 8192
DEPTH = 4

N_MIXERS = 3
NORM_EPS = 1e-6
D_FF = 5632

SSD_EXPAND = 2
SSD_D_INNER = SSD_EXPAND * D_MODEL
SSD_HEAD_DIM = 64
SSD_N_HEADS = SSD_D_INNER // SSD_HEAD_DIM
SSD_N_GROUPS = 8
SSD_HEADS_PER_GROUP = SSD_N_HEADS // SSD_N_GROUPS
SSD_D_STATE = 128
SSD_CONV = 4
SSD_CHUNK = 128
SSD_CONV_DIM = SSD_D_INNER + 2 * SSD_N_GROUPS * SSD_D_STATE
SSD_IN_DIM = SSD_D_INNER + SSD_CONV_DIM + SSD_N_HEADS
SSD_DT_MIN = 1e-3
SSD_DT_MAX = 1e-1

GLA_N_HEADS = 4
GLA_D_K = D_MODEL // 2
GLA_D_V = D_MODEL
GLA_HEAD_K = GLA_D_K // GLA_N_HEADS
GLA_HEAD_V = GLA_D_V // GLA_N_HEADS
GLA_GATE_RANK = 16
GLA_GATE_TAU = 16.0
GLA_CHUNK = 64
GLA_IN_DIM = 2 * GLA_D_K + 2 * GLA_D_V + GLA_GATE_RANK

SGU_WIDTH = 2 * D_MODEL
SGU_N_GROUPS = 8
SGU_GROUP_DIM = SGU_WIDTH // SGU_N_GROUPS
SGU_CHUNK = 128

N_SSD = len(range(0, DEPTH, N_MIXERS))
N_GLA = len(range(1, DEPTH, N_MIXERS))
N_SGU = len(range(2, DEPTH, N_MIXERS))

kernel_name = 'hybrid_ssd_gla_sgu_macaron'


def _tril(n):
    return jnp.tril(jnp.ones((n, n), dtype=bool))


def rms_norm(x, w):
    xf = x.astype(jnp.float32)
    xf = xf * lax.rsqrt(jnp.mean(xf * xf, axis=-1, keepdims=True) + NORM_EPS)
    return (xf * w.astype(jnp.float32)).astype(x.dtype)


def grouped_rms_norm(x, w, groups):
    shp = x.shape
    xg = x.astype(jnp.float32).reshape(shp[:-1] + (groups, shp[-1] // groups))
    xg = xg * lax.rsqrt(jnp.mean(xg * xg, axis=-1, keepdims=True) + NORM_EPS)
    return (xg.reshape(shp) * w.astype(jnp.float32)).astype(x.dtype)


def swiglu_ffn(x, w_in, w_out):
    gate, up = jnp.split(x @ w_in, 2, axis=-1)
    return (jax.nn.silu(gate) * up) @ w_out


def causal_depthwise_conv(x, w, b):
    k_width = w.shape[-1]
    seq = x.shape[1]
    xp = jnp.pad(x, ((0, 0), (k_width - 1, 0), (0, 0)))
    y = b
    for k in range(k_width):
        y = y + xp[:, k:k + seq, :] * w[:, k]
    return y


def ssd_mixer(h, w_in, conv_w, conv_b, dt_bias, a_log, d_skip, norm_w, w_out):
    f32 = jnp.float32
    b, l, _ = h.shape
    Q, G, J, P, N = SSD_CHUNK, SSD_N_GROUPS, SSD_HEADS_PER_GROUP, SSD_HEAD_DIM, SSD_D_STATE
    c = l // Q
    z, xbc, dt = jnp.split(h @ w_in, [SSD_D_INNER, SSD_D_INNER + SSD_CONV_DIM], axis=-1)
    xbc = jax.nn.silu(causal_depthwise_conv(xbc, conv_w, conv_b))
    xs, bm, cm = jnp.split(xbc, [SSD_D_INNER, SSD_D_INNER + G * N], axis=-1)
    xs = xs.astype(f32).reshape(b, c, Q, G, J, P)
    bm = bm.astype(f32).reshape(b, c, Q, G, N)
    cm = cm.astype(f32).reshape(b, c, Q, G, N)
    dt = jax.nn.softplus(dt.astype(f32) + dt_bias.astype(f32)).reshape(b, c, Q, G, J)
    a = -jnp.exp(a_log.astype(f32)).reshape(G, J)
    acum = jnp.cumsum((dt * a).transpose(0, 1, 3, 4, 2), axis=-1)
    xdt = xs * dt[..., None]
    seg = acum[..., :, None] - acum[..., None, :]
    decay = jnp.exp(jnp.where(_tril(Q), seg, -jnp.inf))
    cb = jnp.einsum('bctgn,bcsgn->bcgts', cm, bm)
    scores = cb[:, :, :, None] * decay
    y_diag = jnp.einsum('bcgjts,bcsgjp->bctgjp', scores, xdt)
    decay_to_end = jnp.exp(acum[..., -1:] - acum)
    chunk_states = jnp.einsum('bcsgn,bcgjs,bcsgjp->bcgjpn', bm, decay_to_end, xdt)
    chunk_decay = jnp.exp(acum[..., -1])

    def step(state, inp):
        s_c, a_c = inp
        return state * a_c[..., None, None] + s_c, state

    _, prev = lax.scan(step, jnp.zeros((b, G, J, P, N), f32),
                       (jnp.moveaxis(chunk_states, 1, 0), jnp.moveaxis(chunk_decay, 1, 0)))
    prev = jnp.moveaxis(prev, 0, 1)
    y_off = jnp.einsum('bctgn,bcgjpn,bcgjt->bctgjp', cm, prev, jnp.exp(acum))
    y = y_diag + y_off + d_skip.astype(f32).reshape(G, J, 1) * xs
    y = y.reshape(b, l, SSD_D_INNER)
    y = grouped_rms_norm(y * jax.nn.silu(z.astype(f32)), norm_w, SSD_N_GROUPS)
    return y.astype(h.dtype) @ w_out


def gla_mixer(h, w_in, w_gate2, b_gate, norm_w, w_out):
    f32 = jnp.float32
    b, l, _ = h.shape
    H, K, V, Q = GLA_N_HEADS, GLA_HEAD_K, GLA_HEAD_V, GLA_CHUNK
    c = l // Q
    q, k, v, r, g_low = jnp.split(h @ w_in, [GLA_D_K, 2 * GLA_D_K, 2 * GLA_D_K + GLA_D_V,
                                             2 * GLA_D_K + 2 * GLA_D_V], axis=-1)
    log_a = jax.nn.log_sigmoid((g_low @ w_gate2 + b_gate).astype(f32)) / GLA_GATE_TAU

    def heads(t, d):
        return t.astype(f32).reshape(b, c, Q, H, d).transpose(0, 3, 1, 2, 4)

    q = heads(q, K) * (K ** -0.5)
    k = heads(k, K)
    v = heads(v, V)
    bcum = jnp.cumsum(heads(log_a, K), axis=3)
    ref = bcum[:, :, :, Q // 2:Q // 2 + 1, :]
    att = jnp.einsum('bhctk,bhcsk->bhcts', q * jnp.exp(bcum - ref), k * jnp.exp(ref - bcum))
    att = jnp.where(_tril(Q), att, 0.0)
    o_intra = jnp.einsum('bhcts,bhcsv->bhctv', att, v)
    q_in = q * jnp.exp(bcum)
    k_out = k * jnp.exp(bcum[..., -1:, :] - bcum)
    a_end = jnp.exp(bcum[..., -1, :])

    def step(S, inp):
        qc, kc, vc, ac = inp
        o = jnp.einsum('bhtk,bhkv->bhtv', qc, S)
        S = S * ac[..., None] + jnp.einsum('bhsk,bhsv->bhkv', kc, vc)
        return S, o

    xs = (jnp.moveaxis(q_in, 2, 0), jnp.moveaxis(k_out, 2, 0), jnp.moveaxis(v, 2, 0), jnp.moveaxis(a_end, 2, 0))
    _, o_inter = lax.scan(step, jnp.zeros((b, H, K, V), f32), xs)
    o = o_intra + jnp.moveaxis(o_inter, 0, 2)
    o = o.transpose(0, 2, 3, 1, 4).reshape(b, l, H, V)
    o = rms_norm(o, norm_w).reshape(b, l, GLA_D_V) * jax.nn.silu(r.astype(f32))
    return o.astype(h.dtype) @ w_out


def sgu_mixer(h, w_in, b_in, norm_w, w_s, b_s, w_out):
    b, l, _ = h.shape
    c = l // SGU_CHUNK
    zz = jax.nn.gelu(h @ w_in + b_in, approximate=False)
    u, v = jnp.split(zz, 2, axis=-1)
    v = rms_norm(v, norm_w).reshape(b, c, SGU_CHUNK, SGU_N_GROUPS, SGU_GROUP_DIM)
    w_causal = jnp.where(_tril(SGU_CHUNK), w_s, 0.0)
    sv = jnp.einsum('gts,bcsgd->bctgd', w_causal, v) + b_s.T[:, :, None]
    y = u * sv.reshape(b, l, SGU_WIDTH).astype(u.dtype)
    return y @ w_out


def setup_inputs(seed: int = 0) -> dict:
    key = jax.random.key(seed)
    ks = jax.random.split(key, 25)
    f32 = jnp.float32

    def nrm(k, shape, scale):
        return jax.random.normal(k, shape, f32) * scale

    def gain(k, shape):
        return 1.0 + 0.02 * jax.random.normal(k, shape, f32)

    dt = jnp.exp(jax.random.uniform(ks[8], (N_SSD, SSD_N_HEADS), f32,
                                    math.log(SSD_DT_MIN), math.log(SSD_DT_MAX)))
    return {
        'x': jax.random.normal(ks[0], (BATCH, SEQ, D_MODEL), f32),
        'ffn_norm': gain(ks[1], (DEPTH, 2, D_MODEL)),
        'ffn_w_in': nrm(ks[2], (DEPTH, 2, D_MODEL, 2 * D_FF), D_MODEL ** -0.5),
        'ffn_w_out': nrm(ks[3], (DEPTH, 2, D_FF, D_MODEL), D_FF ** -0.5),
        'mix_norm': gain(ks[4], (DEPTH, D_MODEL)),
        'ssd_w_in': nrm(ks[5], (N_SSD, D_MODEL, SSD_IN_DIM), D_MODEL ** -0.5),
        'ssd_conv_w': nrm(ks[6], (N_SSD, SSD_CONV_DIM, SSD_CONV), SSD_CONV ** -0.5),
        'ssd_conv_b': nrm(ks[7], (N_SSD, SSD_CONV_DIM), 0.02),
        'ssd_dt_bias': dt + jnp.log(-jnp.expm1(-dt)),
        'ssd_a_log': jnp.log(jax.random.uniform(ks[9], (N_SSD, SSD_N_HEADS), f32, 1.0, 16.0)),
        'ssd_d': 1.0 + 0.1 * jax.random.normal(ks[10], (N_SSD, SSD_N_HEADS), f32),
        'ssd_norm': gain(ks[11], (N_SSD, SSD_D_INNER)),
        'ssd_w_out': nrm(ks[12], (N_SSD, SSD_D_INNER, D_MODEL), SSD_D_INNER ** -0.5),
        'gla_w_in': nrm(ks[13], (N_GLA, D_MODEL, GLA_IN_DIM), D_MODEL ** -0.5),
        'gla_w_gate2': nrm(ks[14], (N_GLA, GLA_GATE_RANK, GLA_D_K), GLA_GATE_RANK ** -0.5),
        'gla_b_gate': nrm(ks[15], (N_GLA, GLA_D_K), 0.1),
        'gla_norm': gain(ks[16], (N_GLA, GLA_HEAD_V)),
        'gla_w_out': nrm(ks[17], (N_GLA, GLA_D_V, D_MODEL), GLA_D_V ** -0.5),
        'sgu_w_in': nrm(ks[18], (N_SGU, D_MODEL, 2 * SGU_WIDTH), D_MODEL ** -0.5),
        'sgu_b_in': nrm(ks[19], (N_SGU, 2 * SGU_WIDTH), 0.02),
        'sgu_norm': gain(ks[20], (N_SGU, SGU_WIDTH)),
        'sgu_w_s': nrm(ks[21], (N_SGU, SGU_N_GROUPS, SGU_CHUNK, SGU_CHUNK), SGU_CHUNK ** -0.5),
        'sgu_b_s': 1.0 + 0.1 * jax.random.normal(ks[22], (N_SGU, SGU_N_GROUPS, SGU_CHUNK), f32),
        'sgu_w_out': nrm(ks[23], (N_SGU, SGU_WIDTH, D_MODEL), SGU_WIDTH ** -0.5),
        'final_norm': gain(ks[24], (D_MODEL,)),
    }


def reference(x, ffn_norm, ffn_w_in, ffn_w_out, mix_norm,
              ssd_w_in, ssd_conv_w, ssd_conv_b, ssd_dt_bias, ssd_a_log, ssd_d, ssd_norm, ssd_w_out,
              gla_w_in, gla_w_gate2, gla_b_gate, gla_norm, gla_w_out,
              sgu_w_in, sgu_b_in, sgu_norm, sgu_w_s, sgu_b_s, sgu_w_out,
              final_norm):
    h = x
    for i in range(DEPTH):
        h = h + 0.5 * swiglu_ffn(rms_norm(h, ffn_norm[i, 0]), ffn_w_in[i, 0], ffn_w_out[i, 0])
        hn = rms_norm(h, mix_norm[i])
        kind, j = i % N_MIXERS, i // N_MIXERS
        if kind == 0:
            m = ssd_mixer(hn, ssd_w_in[j], ssd_conv_w[j], ssd_conv_b[j], ssd_dt_bias[j],
                          ssd_a_log[j], ssd_d[j], ssd_norm[j], ssd_w_out[j])
        elif kind == 1:
            m = gla_mixer(hn, gla_w_in[j], gla_w_gate2[j], gla_b_gate[j], gla_norm[j], gla_w_out[j])
        else:
            m = sgu_mixer(hn, sgu_w_in[j], sgu_b_in[j], sgu_norm[j], sgu_w_s[j], sgu_b_s[j], sgu_w_out[j])
        h = h + m
        h = h + 0.5 * swiglu_ffn(rms_norm(h, ffn_norm[i, 1]), ffn_w_in[i, 1], ffn_w_out[i, 1])
    return rms_norm(h, final_norm)
```

```python
import functools

import jax
import jax.numpy as jnp
from jax import lax
from jax.experimental import pallas as pl
from jax.experimental.pallas import tpu as pltpu

F32 = jnp.float32
BF16 = jnp.bfloat16

D_MODEL = 2048
SEQ = 8192
DEPTH = 4
N_MIXERS = 3
NORM_EPS = 1e-6
D_FF = 5632

SSD_D_INNER = 4096
SSD_HEAD_DIM = 64
SSD_N_HEADS = 64
SSD_N_GROUPS = 8
SSD_HEADS_PER_GROUP = 8
SSD_D_STATE = 128
SSD_CONV = 4
SSD_CHUNK = 128
SSD_CONV_DIM = 6144
SSD_GROUP_WIDTH = SSD_HEADS_PER_GROUP * SSD_HEAD_DIM
SSD_MAIN_DIM = SSD_D_INNER + SSD_CONV_DIM

GLA_N_HEADS = 4
GLA_D_K = 1024
GLA_D_V = 2048
GLA_HEAD_K = 256
GLA_HEAD_V = 512
GLA_GATE_RANK = 16
GLA_GATE_TAU = 16.0
GLA_CHUNK = 64
GLA_MAIN_DIM = 2 * GLA_D_K + 2 * GLA_D_V

SGU_WIDTH = 4096
SGU_N_GROUPS = 8
SGU_GROUP_DIM = 512
SGU_CHUNK = 128

FFN_TM = 1024
FFN_TF = 256
PROJ_TM = 1024
PROJ_TN = 512
OUT_TM = 1024
OUT_TN = 1024
OUT_TK = 512
GLA_CHUNKS_PER_STEP = 2
SGU_CHUNKS_PER_STEP = 2
CONV_HALO = 8

VMEM_LIMIT = 56 * 1024 * 1024


def _sigmoid(x):
    return 1.0 / (1.0 + jnp.exp(-x))


def _silu(x):
    return x * _sigmoid(x)


def _softplus(x):
    return jnp.maximum(x, 0.0) + jnp.log1p(jnp.exp(-jnp.abs(x)))


def _rms(x, w):
    return x * lax.rsqrt(jnp.mean(x * x, axis=-1, keepdims=True) + NORM_EPS) * w


def _dot(a, b):
    return jnp.dot(a, b, preferred_element_type=F32)


def _dot_nt(a, b):
    return lax.dot_general(a, b, (((1,), (1,)), ((), ())), preferred_element_type=F32)


def _dot_tn(a, b):
    return lax.dot_general(a, b, (((0,), (0,)), ((), ())), preferred_element_type=F32)


def _split3(x):
    x1 = x.astype(BF16)
    r1 = x - x1.astype(F32)
    x2 = r1.astype(BF16)
    r2 = r1 - x2.astype(F32)
    return x1, x2, r2.astype(BF16)


def _params(*sem):
    return pltpu.CompilerParams(dimension_semantics=sem, vmem_limit_bytes=VMEM_LIMIT)


def _ffn_kernel(h_ref, g_ref, wg_ref, wu_ref, wo_ref, fw_ref, o_ref, n_ref, *, final):
    j = pl.program_id(1)

    @pl.when(j == 0)
    def _():
        x = h_ref[...]
        n_ref[...] = _rms(x, g_ref[...]).astype(BF16)
        o_ref[...] = x

    n = n_ref[...]
    gate = _dot(n, wg_ref[...].astype(BF16))
    up = _dot(n, wu_ref[...].astype(BF16))
    act = (0.5 * _silu(gate) * up).astype(BF16)
    o_ref[...] += _dot(act, wo_ref[...].astype(BF16))

    if final:
        @pl.when(j == pl.num_programs(1) - 1)
        def _():
            o_ref[...] = _rms(o_ref[...], fw_ref[...])


def _ffn(h, ffn_norm4, ffn_w_in, ffn_w_out, final_w, layer, half, final):
    t = h.shape[0]
    nf = D_FF // FFN_TF
    return pl.pallas_call(
        functools.partial(_ffn_kernel, final=final),
        grid=(t // FFN_TM, nf),
        in_specs=[
            pl.BlockSpec((FFN_TM, D_MODEL), lambda i, j: (i, 0), pipeline_mode=pl.Buffered(1)),
            pl.BlockSpec((None, None, 1, D_MODEL), lambda i, j: (layer, half, 0, 0)),
            pl.BlockSpec((None, None, D_MODEL, FFN_TF), lambda i, j: (layer, half, 0, j)),
            pl.BlockSpec((None, None, D_MODEL, FFN_TF), lambda i, j: (layer, half, 0, j + nf)),
            pl.BlockSpec((None, None, FFN_TF, D_MODEL), lambda i, j: (layer, half, j, 0)),
            pl.BlockSpec((1, D_MODEL), lambda i, j: (0, 0)),
        ],
        out_specs=pl.BlockSpec((FFN_TM, D_MODEL), lambda i, j: (i, 0)),
        out_shape=jax.ShapeDtypeStruct((t, D_MODEL), F32),
        scratch_shapes=[pltpu.VMEM((FFN_TM, D_MODEL), BF16)],
        compiler_params=_params("parallel", "arbitrary"),
        name="ffn",
    )(h, ffn_norm4, ffn_w_in, ffn_w_in, ffn_w_out, final_w)


def _mm_res_kernel(a_ref, w_ref, r_ref, o_ref):
    @pl.when(pl.program_id(2) == 0)
    def _():
        o_ref[...] = r_ref[...]

    o_ref[...] += _dot(a_ref[...], w_ref[...].astype(BF16))


def _mm_res(a, w, layer, res):
    t, k = a.shape
    return pl.pallas_call(
        _mm_res_kernel,
        grid=(t // OUT_TM, D_MODEL // OUT_TN, k // OUT_TK),
        in_specs=[
            pl.BlockSpec((OUT_TM, OUT_TK), lambda i, j, kk: (i, kk)),
            pl.BlockSpec((None, OUT_TK, OUT_TN), lambda i, j, kk: (layer, kk, j)),
            pl.BlockSpec((OUT_TM, OUT_TN), lambda i, j, kk: (i, j)),
        ],
        out_specs=pl.BlockSpec((OUT_TM, OUT_TN), lambda i, j, kk: (i, j)),
        out_shape=jax.ShapeDtypeStruct((t, D_MODEL), F32),
        compiler_params=_params("parallel", "parallel", "arbitrary"),
        name="out_proj",
    )(a, w, res)


def _ssd_in_kernel(h_ref, g_ref, w_ref, wdt_ref, wdtt_ref, o_ref, dt_ref, dtt_ref, n_ref):
    @pl.when(pl.program_id(1) == 0)
    def _():
        n = _rms(h_ref[...], g_ref[...]).astype(BF16)
        n_ref[...] = n
        dt_ref[...] = _dot(n, wdt_ref[...].astype(BF16))
        dtt_ref[...] = _dot_nt(wdtt_ref[...].astype(BF16), n)

    o_ref[...] = _dot(n_ref[...], w_ref[...].astype(BF16)).astype(o_ref.dtype)


def _ssd_in(h, mix_norm3, ssd_w_in, w_dt, w_dt_t, layer, j):
    t = h.shape[0]
    return pl.pallas_call(
        _ssd_in_kernel,
        grid=(t // PROJ_TM, SSD_MAIN_DIM // PROJ_TN),
        in_specs=[
            pl.BlockSpec((PROJ_TM, D_MODEL), lambda i, c: (i, 0)),
            pl.BlockSpec((None, 1, D_MODEL), lambda i, c: (layer, 0, 0)),
            pl.BlockSpec((None, D_MODEL, PROJ_TN), lambda i, c: (j, 0, c)),
            pl.BlockSpec((D_MODEL, SSD_N_HEADS), lambda i, c: (0, 0)),
            pl.BlockSpec((SSD_N_HEADS, D_MODEL), lambda i, c: (0, 0)),
        ],
        out_specs=[
            pl.BlockSpec((PROJ_TM, PROJ_TN), lambda i, c: (i, c)),
            pl.BlockSpec((PROJ_TM, SSD_N_HEADS), lambda i, c: (i, 0)),
            pl.BlockSpec((SSD_N_HEADS, PROJ_TM), lambda i, c: (0, i)),
        ],
        out_shape=[
            jax.ShapeDtypeStruct((t, SSD_MAIN_DIM), BF16),
            jax.ShapeDtypeStruct((t, SSD_N_HEADS), F32),
            jax.ShapeDtypeStruct((SSD_N_HEADS, t), F32),
        ],
        scratch_shapes=[pltpu.VMEM((PROJ_TM, D_MODEL), BF16)],
        compiler_params=_params("parallel", "arbitrary"),
        name="ssd_in",
    )(h, mix_norm3, ssd_w_in, w_dt, w_dt_t)


def _ssd_core_kernel(z_ref, xs_ref, bc_ref, dt_ref, dtt_ref, cw_ref, cb_ref, dtb_ref, dtbt_ref,
                     al_ref, alt_ref, dex_ref, nw_ref, e_ref, o_ref, st_ref, cbuf_ref):
    q = SSD_CHUNK
    gw = SSD_GROUP_WIDTH
    ns = SSD_D_STATE

    @pl.when(pl.program_id(0) == 0)
    def _():
        st_ref[...] = jnp.zeros_like(st_ref)
        cbuf_ref[0:CONV_HALO, :] = jnp.zeros((CONV_HALO, SSD_CONV_DIM), F32)

    cbuf_ref[CONV_HALO:CONV_HALO + q, 0:SSD_D_INNER] = xs_ref[...].astype(F32)
    cbuf_ref[CONV_HALO:CONV_HALO + q, SSD_D_INNER:SSD_CONV_DIM] = bc_ref[...].astype(F32)

    def conv(lo, width):
        acc = cb_ref[:, lo:lo + width]
        for k in range(SSD_CONV):
            r0 = CONV_HALO - (SSD_CONV - 1) + k
            acc = acc + cbuf_ref[r0:r0 + q, lo:lo + width] * cw_ref[k:k + 1, lo:lo + width]
        return _silu(acc)

    row = lax.broadcasted_iota(jnp.int32, (q, q), 0)
    col = lax.broadcasted_iota(jnp.int32, (q, q), 1)
    tril = row >= col
    lower = jnp.where(tril, 1.0, 0.0).astype(BF16)
    upper = jnp.where(row <= col, 1.0, 0.0).astype(BF16)

    dt = _softplus(dt_ref[...] + dtb_ref[...])
    da = dt * (-jnp.exp(al_ref[...]))
    acum = sum(_dot(lower, p) for p in _split3(da))
    dtt = _softplus(dtt_ref[...] + dtbt_ref[...])
    dat = dtt * (-jnp.exp(alt_ref[...]))
    acum_t = sum(_dot(p, upper) for p in _split3(dat))
    acum3 = _split3(acum)
    dt16 = dt.astype(BF16)

    lane = lax.broadcasted_iota(jnp.int32, (q, gw), 1)
    even_head = (lane & SSD_HEAD_DIM) == 0

    for g in range(SSD_N_GROUPS):
        c0 = g * gw
        eg = e_ref[:, c0:c0 + gw]
        ae = sum(_dot(p, eg) for p in acum3)
        dte = _dot(dt16, eg)
        xs = conv(c0, gw)
        bg = conv(SSD_D_INNER + g * ns, ns).astype(BF16)
        cg = conv(SSD_D_INNER + SSD_N_GROUPS * ns + g * ns, ns).astype(BF16)
        cbm = _dot_nt(cg, bg)
        xdt = xs * dte
        x_even = jnp.where(even_head, xdt, 0.0).astype(BF16)
        x_odd = jnp.where(even_head, 0.0, xdt).astype(BF16)

        st = st_ref[g]
        y = _dot(cg, st.astype(BF16)) * jnp.exp(ae)

        pieces = []
        for pr in range(SSD_HEADS_PER_GROUP // 2):
            def scores(hh):
                seg = acum[:, hh:hh + 1] - acum_t[hh:hh + 1, :]
                return (cbm * jnp.exp(jnp.where(tril, seg, -jnp.inf))).astype(BF16)

            h0 = g * SSD_HEADS_PER_GROUP + 2 * pr
            lhs = jnp.concatenate([scores(h0), scores(h0 + 1)], axis=1)
            l0 = pr * 2 * SSD_HEAD_DIM
            rhs = jnp.concatenate([x_even[:, l0:l0 + 2 * SSD_HEAD_DIM],
                                   x_odd[:, l0:l0 + 2 * SSD_HEAD_DIM]], axis=0)
            pieces.append(_dot(lhs, rhs))
        y = y + jnp.concatenate(pieces, axis=1)

        last = ae[q - 1:q, :]
        to_end = jnp.exp(last - ae)
        st_ref[g] = st * jnp.exp(last) + _dot_tn(bg, (xdt * to_end).astype(BF16))

        y = y + dex_ref[:, c0:c0 + gw] * xs
        gated = y * _silu(z_ref[:, c0:c0 + gw].astype(F32))
        o_ref[:, c0:c0 + gw] = _rms(gated, nw_ref[:, c0:c0 + gw]).astype(o_ref.dtype)

    cbuf_ref[0:CONV_HALO, :] = cbuf_ref[q:q + CONV_HALO, :]


def _ssd_core(zx, dt, dtt, conv_w_t, conv_b, dt_bias, a_log, d_exp, norm_w, expand):
    t = zx.shape[0]
    q = SSD_CHUNK
    full = lambda shape: pl.BlockSpec(shape, lambda c: (0,) * len(shape))
    return pl.pallas_call(
        _ssd_core_kernel,
        grid=(t // q,),
        in_specs=[
            pl.BlockSpec((q, SSD_D_INNER), lambda c: (c, 0)),
            pl.BlockSpec((q, SSD_D_INNER), lambda c: (c, 1)),
            pl.BlockSpec((q, 2 * SSD_N_GROUPS * SSD_D_STATE), lambda c: (c, 4)),
            pl.BlockSpec((q, SSD_N_HEADS), lambda c: (c, 0)),
            pl.BlockSpec((SSD_N_HEADS, q), lambda c: (0, c)),
            full((SSD_CONV, SSD_CONV_DIM)),
            full((1, SSD_CONV_DIM)),
            full((1, SSD_N_HEADS)),
            full((SSD_N_HEADS, 1)),
            full((1, SSD_N_HEADS)),
            full((SSD_N_HEADS, 1)),
            full((1, SSD_D_INNER)),
            full((1, SSD_D_INNER)),
            full((SSD_N_HEADS, SSD_D_INNER)),
        ],
        out_specs=pl.BlockSpec((q, SSD_D_INNER), lambda c: (c, 0)),
        out_shape=jax.ShapeDtypeStruct((t, SSD_D_INNER), BF16),
        scratch_shapes=[
            pltpu.VMEM((SSD_N_GROUPS, SSD_D_STATE, SSD_GROUP_WIDTH), F32),
            pltpu.VMEM((CONV_HALO + q, SSD_CONV_DIM), F32),
        ],
        compiler_params=_params("arbitrary"),
        name="ssd_core",
    )(zx, zx, zx, dt, dtt, conv_w_t, conv_b, dt_bias.reshape(1, -1), dt_bias.reshape(-1, 1),
      a_log.reshape(1, -1), a_log.reshape(-1, 1), d_exp, norm_w, expand)


def _ssd_mixer(h, mix_norm3, layer, j, ssd_w_in, ssd_conv_w, ssd_conv_b, ssd_dt_bias, ssd_a_log,
               ssd_d, ssd_norm, ssd_w_out):
    w_dt = ssd_w_in[j, :, SSD_MAIN_DIM:]
    zx, dt, dtt = _ssd_in(h, mix_norm3, ssd_w_in, w_dt, w_dt.T, layer, j)
    head_of_channel = jnp.arange(SSD_D_INNER) // SSD_HEAD_DIM
    expand = (jnp.arange(SSD_N_HEADS)[:, None] == head_of_channel[None, :]).astype(BF16)
    d_exp = jnp.repeat(ssd_d[j], SSD_HEAD_DIM).reshape(1, SSD_D_INNER)
    yn = _ssd_core(zx, dt, dtt, ssd_conv_w[j].T, ssd_conv_b[j].reshape(1, -1), ssd_dt_bias[j],
                   ssd_a_log[j], d_exp, ssd_norm[j].reshape(1, -1), expand)
    return _mm_res(yn, ssd_w_out, j, h)


def _gla_in_kernel(h_ref, g_ref, w_ref, wl_ref, o_ref, gl_ref, n_ref):
    @pl.when(pl.program_id(1) == 0)
    def _():
        n = _rms(h_ref[...], g_ref[...]).astype(BF16)
        n_ref[...] = n
        gl_ref[...] = _dot(n, wl_ref[...].astype(BF16))

    o_ref[...] = _dot(n_ref[...], w_ref[...].astype(BF16)).astype(o_ref.dtype)


def _gla_in(h, mix_norm3, gla_w_in, w_low, layer, j):
    t = h.shape[0]
    return pl.pallas_call(
        _gla_in_kernel,
        grid=(t // PROJ_TM, GLA_MAIN_DIM // PROJ_TN),
        in_specs=[
            pl.BlockSpec((PROJ_TM, D_MODEL), lambda i, c: (i, 0)),
            pl.BlockSpec((None, 1, D_MODEL), lambda i, c: (layer, 0, 0)),
            pl.BlockSpec((None, D_MODEL, PROJ_TN), lambda i, c: (j, 0, c)),
            pl.BlockSpec((D_MODEL, GLA_GATE_RANK), lambda i, c: (0, 0)),
        ],
        out_specs=[
            pl.BlockSpec((PROJ_TM, PROJ_TN), lambda i, c: (i, c)),
            pl.BlockSpec((PROJ_TM, GLA_GATE_RANK), lambda i, c: (i, 0)),
        ],
        out_shape=[
            jax.ShapeDtypeStruct((t, GLA_MAIN_DIM), BF16),
            jax.ShapeDtypeStruct((t, GLA_GATE_RANK), F32),
        ],
        scratch_shapes=[pltpu.VMEM((PROJ_TM, D_MODEL), BF16)],
        compiler_params=_params("parallel", "arbitrary"),
        name="gla_in",
    )(h, mix_norm3, gla_w_in, w_low)


def _gla_core_kernel(x_ref, gl_ref, w2_ref, bg_ref, nw_ref, o_ref, st_ref):
    q = GLA_CHUNK
    hk, hv = GLA_HEAD_K, GLA_HEAD_V

    @pl.when(pl.program_id(0) == 0)
    def _():
        st_ref[...] = jnp.zeros_like(st_ref)

    row = lax.broadcasted_iota(jnp.int32, (q, q), 0)
    col = lax.broadcasted_iota(jnp.int32, (q, q), 1)
    tril = row >= col
    lower = jnp.where(tril, 1.0, 0.0).astype(BF16)
    w2 = w2_ref[...].astype(BF16)

    for ci in range(GLA_CHUNKS_PER_STEP):
        r0 = ci * q
        lg = _dot(gl_ref[r0:r0 + q, :].astype(BF16), w2) + bg_ref[...]
        log_a = (jnp.minimum(lg, 0.0) - jnp.log1p(jnp.exp(-jnp.abs(lg)))) * (1.0 / GLA_GATE_TAU)
        bcum = sum(_dot(lower, p) for p in _split3(log_a))
        for hd in range(GLA_N_HEADS):
            k0 = hd * hk
            v0 = 2 * GLA_D_K + hd * hv
            qh = x_ref[r0:r0 + q, k0:k0 + hk].astype(F32) * (hk ** -0.5)
            kh = x_ref[r0:r0 + q, GLA_D_K + k0:GLA_D_K + k0 + hk].astype(F32)
            vh = x_ref[r0:r0 + q, v0:v0 + hv]
            rh = x_ref[r0:r0 + q, v0 + GLA_D_V:v0 + GLA_D_V + hv].astype(F32)
            bc = bcum[:, k0:k0 + hk]
            mid = bc[q // 2:q // 2 + 1, :]
            end = bc[q - 1:q, :]
            att = _dot_nt((qh * jnp.exp(bc - mid)).astype(BF16), (kh * jnp.exp(mid - bc)).astype(BF16))
            att = jnp.where(tril, att, 0.0)
            o = _dot(att.astype(BF16), vh)
            st = st_ref[hd]
            o = o + _dot_nt((qh * jnp.exp(bc)).astype(BF16), st.astype(BF16))
            k_out = (kh * jnp.exp(end - bc)).astype(BF16)
            st_ref[hd] = st * jnp.exp(end) + _dot_tn(vh, k_out)
            o = _rms(o, nw_ref[...]) * _silu(rh)
            o_ref[r0:r0 + q, hd * hv:(hd + 1) * hv] = o.astype(o_ref.dtype)


def _gla_core(qkvr, g_low, w_gate2, b_gate, norm_w):
    t = qkvr.shape[0]
    rows = GLA_CHUNK * GLA_CHUNKS_PER_STEP
    return pl.pallas_call(
        _gla_core_kernel,
        grid=(t // rows,),
        in_specs=[
            pl.BlockSpec((rows, GLA_MAIN_DIM), lambda c: (c, 0)),
            pl.BlockSpec((rows, GLA_GATE_RANK), lambda c: (c, 0)),
            pl.BlockSpec((GLA_GATE_RANK, GLA_D_K), lambda c: (0, 0)),
            pl.BlockSpec((1, GLA_D_K), lambda c: (0, 0)),
            pl.BlockSpec((1, GLA_HEAD_V), lambda c: (0, 0)),
        ],
        out_specs=pl.BlockSpec((rows, GLA_D_V), lambda c: (c, 0)),
        out_shape=jax.ShapeDtypeStruct((t, GLA_D_V), BF16),
        scratch_shapes=[pltpu.VMEM((GLA_N_HEADS, GLA_HEAD_V, GLA_HEAD_K), F32)],
        compiler_params=_params("arbitrary"),
        name="gla_core",
    )(qkvr, g_low, w_gate2, b_gate, norm_w)


def _gla_mixer(h, mix_norm3, layer, j, gla_w_in, gla_w_gate2, gla_b_gate, gla_norm, gla_w_out):
    w_low = gla_w_in[j, :, GLA_MAIN_DIM:]
    qkvr, g_low = _gla_in(h, mix_norm3, gla_w_in, w_low, layer, j)
    o = _gla_core(qkvr, g_low, gla_w_gate2[j], gla_b_gate[j].reshape(1, -1), gla_norm[j].reshape(1, -1))
    return _mm_res(o, gla_w_out, j, h)


def _sgu_in_kernel(h_ref, g_ref, w_ref, b_ref, o_ref, n_ref):
    @pl.when(pl.program_id(1) == 0)
    def _():
        n_ref[...] = _rms(h_ref[...], g_ref[...]).astype(BF16)

    y = _dot(n_ref[...], w_ref[...].astype(BF16)) + b_ref[...]
    o_ref[...] = (0.5 * y * (1.0 + lax.erf(y * (0.5 ** 0.5)))).astype(o_ref.dtype)


def _sgu_in(h, mix_norm3, sgu_w_in, sgu_b_in3, layer, j):
    t = h.shape[0]
    return pl.pallas_call(
        _sgu_in_kernel,
        grid=(t // PROJ_TM, 2 * SGU_WIDTH // PROJ_TN),
        in_specs=[
            pl.BlockSpec((PROJ_TM, D_MODEL), lambda i, c: (i, 0)),
            pl.BlockSpec((None, 1, D_MODEL), lambda i, c: (layer, 0, 0)),
            pl.BlockSpec((None, D_MODEL, PROJ_TN), lambda i, c: (j, 0, c)),
            pl.BlockSpec((None, 1, PROJ_TN), lambda i, c: (j, 0, c)),
        ],
        out_specs=pl.BlockSpec((PROJ_TM, PROJ_TN), lambda i, c: (i, c)),
        out_shape=jax.ShapeDtypeStruct((t, 2 * SGU_WIDTH), BF16),
        scratch_shapes=[pltpu.VMEM((PROJ_TM, D_MODEL), BF16)],
        compiler_params=_params("parallel", "arbitrary"),
        name="sgu_in",
    )(h, mix_norm3, sgu_w_in, sgu_b_in3)


def _sgu_core_kernel(u_ref, v_ref, nw_ref, ws_ref, bs_ref, o_ref):
    q = SGU_CHUNK
    gd = SGU_GROUP_DIM
    row = lax.broadcasted_iota(jnp.int32, (q, q), 0)
    col = lax.broadcasted_iota(jnp.int32, (q, q), 1)
    tril = row >= col
    for ci in range(SGU_CHUNKS_PER_STEP):
        r0 = ci * q
        vn = _rms(v_ref[r0:r0 + q, :].astype(F32), nw_ref[...]).astype(BF16)
        for g in range(SGU_N_GROUPS):
            c0 = g * gd
            wc = jnp.where(tril, ws_ref[g], 0.0).astype(BF16)
            sv = _dot(wc, vn[:, c0:c0 + gd]) + bs_ref[:, g:g + 1]
            o_ref[r0:r0 + q, c0:c0 + gd] = (u_ref[r0:r0 + q, c0:c0 + gd].astype(F32) * sv).astype(o_ref.dtype)


def _sgu_core(zz, norm_w, w_s, b_s_t):
    t = zz.shape[0]
    rows = SGU_CHUNK * SGU_CHUNKS_PER_STEP
    return pl.pallas_call(
        _sgu_core_kernel,
        grid=(t // rows,),
        in_specs=[
            pl.BlockSpec((rows, SGU_WIDTH), lambda c: (c, 0)),
            pl.BlockSpec((rows, SGU_WIDTH), lambda c: (c, 1)),
            pl.BlockSpec((1, SGU_WIDTH), lambda c: (0, 0)),
            pl.BlockSpec((SGU_N_GROUPS, SGU_CHUNK, SGU_CHUNK), lambda c: (0, 0, 0)),
            pl.BlockSpec((SGU_CHUNK, SGU_N_GROUPS), lambda c: (0, 0)),
        ],
        out_specs=pl.BlockSpec((rows, SGU_WIDTH), lambda c: (c, 0)),
        out_shape=jax.ShapeDtypeStruct((t, SGU_WIDTH), BF16),
        compiler_params=_params("parallel"),
        name="sgu_core",
    )(zz, zz, norm_w, w_s, b_s_t)


def _sgu_mixer(h, mix_norm3, layer, j, sgu_w_in, sgu_b_in, sgu_norm, sgu_w_s, sgu_b_s, sgu_w_out):
    zz = _sgu_in(h, mix_norm3, sgu_w_in, sgu_b_in.reshape(sgu_b_in.shape[0], 1, -1), layer, j)
    y = _sgu_core(zz, sgu_norm[j].reshape(1, -1), sgu_w_s[j], sgu_b_s[j].T)
    return _mm_res(y, sgu_w_out, j, h)


def kernel(x, ffn_norm, ffn_w_in, ffn_w_out, mix_norm, ssd_w_in, ssd_conv_w, ssd_conv_b, ssd_dt_bias,
           ssd_a_log, ssd_d, ssd_norm, ssd_w_out, gla_w_in, gla_w_gate2, gla_b_gate, gla_norm, gla_w_out,
           sgu_w_in, sgu_b_in, sgu_norm, sgu_w_s, sgu_b_s, sgu_w_out, final_norm):
    b, t, d = x.shape
    assert (b, t, d) == (1, SEQ, D_MODEL)
    h = x.reshape(t, d)
    ffn_norm4 = ffn_norm.reshape(DEPTH, 2, 1, D_MODEL)
    mix_norm3 = mix_norm.reshape(DEPTH, 1, D_MODEL)
    final_w = final_norm.reshape(1, D_MODEL)
    for i in range(DEPTH):
        h = _ffn(h, ffn_norm4, ffn_w_in, ffn_w_out, final_w, i, 0, False)
        kind, j = i % N_MIXERS, i // N_MIXERS
        if kind == 0:
            h = _ssd_mixer(h, mix_norm3, i, j, ssd_w_in, ssd_conv_w, ssd_conv_b, ssd_dt_bias, ssd_a_log,
                           ssd_d, ssd_norm, ssd_w_out)
        elif kind == 1:
            h = _gla_mixer(h, mix_norm3, i, j, gla_w_in, gla_w_gate2, gla_b_gate, gla_norm, gla_w_out)
        else:
            h = _sgu_mixer(h, mix_norm3, i, j, sgu_w_in, sgu_b_in, sgu_norm, sgu_w_s, sgu_b_s, sgu_w_out)
        h = _ffn(h, ffn_norm4, ffn_w_in, ffn_w_out, final_w, i, 1, i == DEPTH - 1)
    return h.reshape(b, t, d)
```

```python
import functools

import jax
import jax.numpy as jnp
import numpy as np
from jax import lax
from jax.experimental import pallas as pl
from jax.experimental.pallas import tpu as pltpu

F32 = jnp.float32
BF16 = jnp.bfloat16

D_MODEL = 2048
SEQ = 8192
DEPTH = 4
N_MIXERS = 3
NORM_EPS = 1e-6
D_FF = 5632

SSD_D_INNER = 4096
SSD_HEAD_DIM = 64
SSD_N_HEADS = 64
SSD_N_GROUPS = 8
SSD_HEADS_PER_GROUP = 8
SSD_D_STATE = 128
SSD_CONV = 4
SSD_CHUNK = 128
SSD_CONV_DIM = 6144
SSD_GROUP_WIDTH = SSD_HEADS_PER_GROUP * SSD_HEAD_DIM
SSD_MAIN_DIM = SSD_D_INNER + SSD_CONV_DIM

GLA_N_HEADS = 4
GLA_D_K = 1024
GLA_D_V = 2048
GLA_HEAD_K = 256
GLA_HEAD_V = 512
GLA_GATE_RANK = 16
GLA_GATE_TAU = 16.0
GLA_CHUNK = 64
GLA_MAIN_DIM = 2 * GLA_D_K + 2 * GLA_D_V

SGU_WIDTH = 4096
SGU_N_GROUPS = 8
SGU_GROUP_DIM = 512
SGU_CHUNK = 128

FFN_TM = 1024
FFN_TF = 256
PROJ_TM = 1024
PROJ_TN = 1024
OUT_TM = 1024
OUT_TN = 512
GLA_CHUNKS_PER_STEP = 2
SGU_CHUNKS_PER_STEP = 2
LANES = 128
BF16_ROWS = 16
CONV_TAIL = BF16_ROWS
CONV_K = -(-(SSD_CONV - 1) * (SSD_CHUNK + CONV_TAIL) // LANES) * LANES
LOG2E = 1.4426950408889634

VMEM_LIMIT = 56 * 1024 * 1024


def _silu(x):
    hx = 0.5 * x
    return hx * jnp.tanh(hx) + hx


def _softplus(x):
    return jnp.maximum(x, 0.0) + jnp.log1p(jnp.exp(-jnp.abs(x)))


def _rms(x, w):
    return x * lax.rsqrt(jnp.mean(x * x, axis=-1, keepdims=True) + NORM_EPS) * w


def _dot(a, b):
    return jnp.dot(a, b, preferred_element_type=F32)


def _dot_nt(a, b):
    return lax.dot_general(a, b, (((1,), (1,)), ((), ())), preferred_element_type=F32)


def _dot_tn(a, b):
    return lax.dot_general(a, b, (((0,), (0,)), ((), ())), preferred_element_type=F32)


def _split3(x):
    x1 = x.astype(BF16)
    r1 = x - x1.astype(F32)
    x2 = r1.astype(BF16)
    r2 = r1 - x2.astype(F32)
    return x1, x2, r2.astype(BF16)


def _params(*sem):
    return pltpu.CompilerParams(dimension_semantics=sem, vmem_limit_bytes=VMEM_LIMIT)


def _ffn_kernel(h_ref, g_ref, wg_ref, wu_ref, wo_ref, fw_ref, o_ref, n_ref, *, final):
    j = pl.program_id(1)

    @pl.when(j == 0)
    def _():
        x = h_ref[...]
        n_ref[...] = _rms(x, g_ref[...]).astype(BF16)
        o_ref[...] = x

    n = n_ref[...]
    gate = _dot(n, wg_ref[...].astype(BF16))
    up = _dot(n, wu_ref[...].astype(BF16))
    act = (0.5 * _silu(gate) * up).astype(BF16)
    o_ref[...] += _dot(act, wo_ref[...].astype(BF16))

    if final:
        @pl.when(j == pl.num_programs(1) - 1)
        def _():
            o_ref[...] = _rms(o_ref[...], fw_ref[...])


def _ffn(h, ffn_norm4, ffn_w_in, ffn_w_out, final_w, layer, half, final):
    t = h.shape[0]
    nf = D_FF // FFN_TF
    return pl.pallas_call(
        functools.partial(_ffn_kernel, final=final),
        grid=(t // FFN_TM, nf),
        in_specs=[
            pl.BlockSpec((FFN_TM, D_MODEL), lambda i, j: (i, 0), pipeline_mode=pl.Buffered(1)),
            pl.BlockSpec((None, None, 1, D_MODEL), lambda i, j: (layer, half, 0, 0)),
            pl.BlockSpec((None, None, D_MODEL, FFN_TF), lambda i, j: (layer, half, 0, j)),
            pl.BlockSpec((None, None, D_MODEL, FFN_TF), lambda i, j: (layer, half, 0, j + nf)),
            pl.BlockSpec((None, None, FFN_TF, D_MODEL), lambda i, j: (layer, half, j, 0)),
            pl.BlockSpec((1, D_MODEL), lambda i, j: (0, 0)),
        ],
        out_specs=pl.BlockSpec((FFN_TM, D_MODEL), lambda i, j: (i, 0)),
        out_shape=jax.ShapeDtypeStruct((t, D_MODEL), F32),
        scratch_shapes=[pltpu.VMEM((FFN_TM, D_MODEL), BF16)],
        compiler_params=_params("parallel", "arbitrary"),
        name="ffn",
    )(h, ffn_norm4, ffn_w_in, ffn_w_in, ffn_w_out, final_w)


def _mm_res_kernel(a_ref, w_ref, r_ref, o_ref, wb_ref):
    @pl.when(pl.program_id(1) == 0)
    def _():
        wb_ref[...] = w_ref[...].astype(BF16)

    o_ref[...] = r_ref[...] + _dot(a_ref[...], wb_ref[...])


def _mm_res(a, w, layer, res):
    t, k = a.shape
    return pl.pallas_call(
        _mm_res_kernel,
        grid=(D_MODEL // OUT_TN, t // OUT_TM),
        in_specs=[
            pl.BlockSpec((OUT_TM, k), lambda j, i: (i, 0)),
            pl.BlockSpec((None, k, OUT_TN), lambda j, i: (layer, 0, j), pipeline_mode=pl.Buffered(1)),
            pl.BlockSpec((OUT_TM, OUT_TN), lambda j, i: (i, j)),
        ],
        out_specs=pl.BlockSpec((OUT_TM, OUT_TN), lambda j, i: (i, j)),
        out_shape=jax.ShapeDtypeStruct((t, D_MODEL), F32),
        scratch_shapes=[pltpu.VMEM((k, OUT_TN), BF16)],
        compiler_params=_params("parallel", "arbitrary"),
        name="out_proj",
    )(a, w, res)


def _narrow_cols(w_ref, valid):
    lane = lax.broadcasted_iota(jnp.int32, (1, LANES), 1)
    return jnp.where(lane < valid, w_ref[...], 0.0).astype(BF16)


def _ssd_in_kernel(h_ref, g_ref, w_ref, wdt_ref, o_ref, dt_ref, n_ref):
    @pl.when(pl.program_id(1) == 0)
    def _():
        n = _rms(h_ref[...], g_ref[...]).astype(BF16)
        n_ref[...] = n
        dt_ref[...] = _dot(n, _narrow_cols(wdt_ref, SSD_N_HEADS))

    o_ref[...] = _dot(n_ref[...], w_ref[...].astype(BF16)).astype(o_ref.dtype)


def _ssd_in(h, mix_norm3, ssd_w_in, layer, j):
    t = h.shape[0]
    return pl.pallas_call(
        _ssd_in_kernel,
        grid=(t // PROJ_TM, SSD_MAIN_DIM // PROJ_TN),
        in_specs=[
            pl.BlockSpec((PROJ_TM, D_MODEL), lambda i, c: (i, 0), pipeline_mode=pl.Buffered(1)),
            pl.BlockSpec((None, 1, D_MODEL), lambda i, c: (layer, 0, 0)),
            pl.BlockSpec((None, D_MODEL, PROJ_TN), lambda i, c: (j, 0, c)),
            pl.BlockSpec((None, D_MODEL, LANES), lambda i, c: (j, 0, SSD_MAIN_DIM // LANES)),
        ],
        out_specs=[
            pl.BlockSpec((PROJ_TM, PROJ_TN), lambda i, c: (i, c)),
            pl.BlockSpec((PROJ_TM, LANES), lambda i, c: (i, 0)),
        ],
        out_shape=[
            jax.ShapeDtypeStruct((t, SSD_MAIN_DIM), BF16),
            jax.ShapeDtypeStruct((t, LANES), F32),
        ],
        scratch_shapes=[pltpu.VMEM((PROJ_TM, D_MODEL), BF16)],
        compiler_params=_params("parallel", "arbitrary"),
        name="ssd_in",
    )(h, mix_norm3, ssd_w_in, ssd_w_in)


def _conv_shift_matrix():
    q, blk = SSD_CHUNK, SSD_CHUNK + CONV_TAIL
    s = np.zeros((q, CONV_K), np.float32)
    for k in range(SSD_CONV - 1):
        shift = SSD_CONV - 1 - k
        for t in range(q):
            src = t - shift
            s[t, k * blk + (src if src >= 0 else q + CONV_TAIL + src)] = 1.0
    return jnp.asarray(s, BF16)


def _ssd_core_kernel(z_ref, xs_ref, bc_ref, dt_ref, cw_ref, cb_ref, dtb_ref, al_ref, dex_ref, nw_ref,
                     e_ref, e3_ref, sh_ref, o_ref, st_ref, tail_ref):
    q = SSD_CHUNK
    gw = SSD_GROUP_WIDTH
    ns = SSD_D_STATE

    @pl.when(pl.program_id(0) == 0)
    def _():
        st_ref[...] = jnp.zeros_like(st_ref)
        tail_ref[...] = jnp.zeros_like(tail_ref)

    def conv(src_ref, src_lo, lo, width):
        cur = src_ref[:, src_lo:src_lo + width]
        tail = tail_ref[:, lo:lo + width]
        parts = []
        for k in range(SSD_CONV - 1):
            wk = cw_ref[k:k + 1, lo:lo + width].astype(BF16)
            parts += [cur * wk, tail * wk]
        parts.append(jnp.zeros((CONV_K - (SSD_CONV - 1) * (q + CONV_TAIL), width), BF16))
        acc = _dot(sh_ref[...], jnp.concatenate(parts, axis=0))
        acc = acc + cur.astype(F32) * cw_ref[SSD_CONV - 1:SSD_CONV, lo:lo + width] + cb_ref[:, lo:lo + width]
        return _silu(acc)

    row = lax.broadcasted_iota(jnp.int32, (q, q), 0)
    col = lax.broadcasted_iota(jnp.int32, (q, q), 1)
    tril = row >= col
    lower = jnp.where(tril, 1.0, 0.0).astype(BF16)

    dt = _softplus(dt_ref[...] + dtb_ref[...])
    da = dt * (-LOG2E * jnp.exp(al_ref[...]))
    acum = sum(_dot(lower, p) for p in _split3(da))
    acum_t = acum.T
    acum3 = jnp.concatenate(_split3(acum), axis=1)
    dt16 = dt.astype(BF16)

    lane = lax.broadcasted_iota(jnp.int32, (q, 2 * SSD_HEAD_DIM), 1)
    even_head = lane < SSD_HEAD_DIM

    for g in range(SSD_N_GROUPS):
        c0 = g * gw
        ae = _dot(acum3, e3_ref[:, c0:c0 + gw])
        dte = _dot(dt16, e_ref[:, c0:c0 + gw])
        xs = conv(xs_ref, c0, c0, gw)
        b0 = g * ns
        c1 = SSD_N_GROUPS * ns + g * ns
        bg = conv(bc_ref, b0, SSD_D_INNER + b0, ns).astype(BF16)
        cg = conv(bc_ref, c1, SSD_D_INNER + c1, ns).astype(BF16)
        cbm = _dot_nt(cg, bg)
        xdt = xs * dte

        st = st_ref[g]
        y = _dot(cg, st.astype(BF16)) * jnp.exp2(ae)

        pieces = []
        for pr in range(SSD_HEADS_PER_GROUP // 2):
            def scores(hh):
                seg = acum[:, hh:hh + 1] - acum_t[hh:hh + 1, :]
                return (cbm * jnp.exp2(jnp.where(tril, seg, -jnp.inf))).astype(BF16)

            h0 = g * SSD_HEADS_PER_GROUP + 2 * pr
            lhs = jnp.concatenate([scores(h0), scores(h0 + 1)], axis=1)
            l0 = pr * 2 * SSD_HEAD_DIM
            xp = xdt[:, l0:l0 + 2 * SSD_HEAD_DIM]
            rhs = jnp.concatenate([jnp.where(even_head, xp, 0.0),
                                   jnp.where(even_head, 0.0, xp)], axis=0)
            pieces.append(_dot(lhs, rhs.astype(BF16)))
        y = y + jnp.concatenate(pieces, axis=1)

        last = ae[q - 1:q, :]
        to_end = jnp.exp2(last - ae)
        st_ref[g] = st * jnp.exp2(last) + _dot_tn(bg, (xdt * to_end).astype(BF16))

        y = y + dex_ref[:, c0:c0 + gw] * xs
        gated = y * _silu(z_ref[:, c0:c0 + gw].astype(F32))
        o_ref[:, c0:c0 + gw] = _rms(gated, nw_ref[:, c0:c0 + gw]).astype(o_ref.dtype)

    tail_ref[:, 0:SSD_D_INNER] = xs_ref[q - CONV_TAIL:q, :]
    tail_ref[:, SSD_D_INNER:SSD_CONV_DIM] = bc_ref[q - CONV_TAIL:q, :]


def _ssd_core(zx, dt, conv_w_t, conv_b, dt_bias, a_log, d_exp, norm_w, expand):
    t = zx.shape[0]
    q = SSD_CHUNK
    full = lambda shape: pl.BlockSpec(shape, lambda c: (0,) * len(shape))
    pad_heads = lambda v: jnp.pad(v, (0, LANES - SSD_N_HEADS)).reshape(1, LANES)
    return pl.pallas_call(
        _ssd_core_kernel,
        grid=(t // q,),
        in_specs=[
            pl.BlockSpec((q, SSD_D_INNER), lambda c: (c, 0)),
            pl.BlockSpec((q, SSD_D_INNER), lambda c: (c, 1)),
            pl.BlockSpec((q, 2 * SSD_N_GROUPS * SSD_D_STATE), lambda c: (c, 4)),
            pl.BlockSpec((q, LANES), lambda c: (c, 0)),
            full((SSD_CONV, SSD_CONV_DIM)),
            full((1, SSD_CONV_DIM)),
            full((1, LANES)),
            full((1, LANES)),
            full((1, SSD_D_INNER)),
            full((1, SSD_D_INNER)),
            full((LANES, SSD_D_INNER)),
            full((3 * LANES, SSD_D_INNER)),
            full((q, CONV_K)),
        ],
        out_specs=pl.BlockSpec((q, SSD_D_INNER), lambda c: (c, 0)),
        out_shape=jax.ShapeDtypeStruct((t, SSD_D_INNER), BF16),
        scratch_shapes=[
            pltpu.VMEM((SSD_N_GROUPS, SSD_D_STATE, SSD_GROUP_WIDTH), F32),
            pltpu.VMEM((CONV_TAIL, SSD_CONV_DIM), BF16),
        ],
        compiler_params=_params("arbitrary"),
        name="ssd_core",
    )(zx, zx, zx, dt, conv_w_t, conv_b, pad_heads(dt_bias), pad_heads(a_log), d_exp, norm_w,
      expand, jnp.tile(expand, (3, 1)), _conv_shift_matrix())


def _ssd_mixer(h, mix_norm3, layer, j, ssd_w_in, ssd_conv_w, ssd_conv_b, ssd_dt_bias, ssd_a_log,
               ssd_d, ssd_norm, ssd_w_out):
    zx, dt = _ssd_in(h, mix_norm3, ssd_w_in, layer, j)
    head_of_channel = jnp.arange(SSD_D_INNER) // SSD_HEAD_DIM
    expand = (jnp.arange(LANES)[:, None] == head_of_channel[None, :]).astype(BF16)
    d_exp = jnp.repeat(ssd_d[j], SSD_HEAD_DIM).reshape(1, SSD_D_INNER)
    yn = _ssd_core(zx, dt, ssd_conv_w[j].T, ssd_conv_b[j].reshape(1, -1), ssd_dt_bias[j],
                   ssd_a_log[j], d_exp, ssd_norm[j].reshape(1, -1), expand)
    return _mm_res(yn, ssd_w_out, j, h)


def _gla_in_kernel(h_ref, g_ref, w_ref, wl_ref, o_ref, gl_ref, n_ref):
    @pl.when(pl.program_id(1) == 0)
    def _():
        n = _rms(h_ref[...], g_ref[...]).astype(BF16)
        n_ref[...] = n
        gl_ref[...] = _dot(n, _narrow_cols(wl_ref, GLA_GATE_RANK))

    o_ref[...] = _dot(n_ref[...], w_ref[...].astype(BF16)).astype(o_ref.dtype)


def _gla_in(h, mix_norm3, gla_w_in, layer, j):
    t = h.shape[0]
    return pl.pallas_call(
        _gla_in_kernel,
        grid=(t // PROJ_TM, GLA_MAIN_DIM // PROJ_TN),
        in_specs=[
            pl.BlockSpec((PROJ_TM, D_MODEL), lambda i, c: (i, 0), pipeline_mode=pl.Buffered(1)),
            pl.BlockSpec((None, 1, D_MODEL), lambda i, c: (layer, 0, 0)),
            pl.BlockSpec((None, D_MODEL, PROJ_TN), lambda i, c: (j, 0, c)),
            pl.BlockSpec((None, D_MODEL, LANES), lambda i, c: (j, 0, GLA_MAIN_DIM // LANES)),
        ],
        out_specs=[
            pl.BlockSpec((PROJ_TM, PROJ_TN), lambda i, c: (i, c)),
            pl.BlockSpec((PROJ_TM, LANES), lambda i, c: (i, 0)),
        ],
        out_shape=[
            jax.ShapeDtypeStruct((t, GLA_MAIN_DIM), BF16),
            jax.ShapeDtypeStruct((t, LANES), F32),
        ],
        scratch_shapes=[pltpu.VMEM((PROJ_TM, D_MODEL), BF16)],
        compiler_params=_params("parallel", "arbitrary"),
        name="gla_in",
    )(h, mix_norm3, gla_w_in, gla_w_in)


def _gla_core_kernel(x_ref, gl_ref, w2_ref, bg_ref, nw_ref, o_ref, st_ref):
    q = GLA_CHUNK
    hk, hv = GLA_HEAD_K, GLA_HEAD_V

    @pl.when(pl.program_id(0) == 0)
    def _():
        st_ref[...] = jnp.zeros_like(st_ref)

    row = lax.broadcasted_iota(jnp.int32, (q, q), 0)
    col = lax.broadcasted_iota(jnp.int32, (q, q), 1)
    tril = row >= col
    lower = jnp.where(tril, 1.0, 0.0).astype(BF16)
    w2 = w2_ref[...].astype(BF16)

    for ci in range(GLA_CHUNKS_PER_STEP):
        r0 = ci * q
        lg = _dot(gl_ref[r0:r0 + q, :].astype(BF16), w2) + bg_ref[...]
        log_a = (jnp.minimum(lg, 0.0) - jnp.log1p(jnp.exp(-jnp.abs(lg)))) * (LOG2E / GLA_GATE_TAU)
        bcum = sum(_dot(lower, p) for p in _split3(log_a))
        for hd in range(GLA_N_HEADS):
            k0 = hd * hk
            v0 = 2 * GLA_D_K + hd * hv
            qh = x_ref[r0:r0 + q, k0:k0 + hk].astype(F32) * (hk ** -0.5)
            kh = x_ref[r0:r0 + q, GLA_D_K + k0:GLA_D_K + k0 + hk].astype(F32)
            vh = x_ref[r0:r0 + q, v0:v0 + hv]
            rh = x_ref[r0:r0 + q, v0 + GLA_D_V:v0 + GLA_D_V + hv].astype(F32)
            bc = bcum[:, k0:k0 + hk]
            mid = bc[q // 2:q // 2 + 1, :]
            end = bc[q - 1:q, :]
            att = _dot_nt((qh * jnp.exp2(bc - mid)).astype(BF16), (kh * jnp.exp2(mid - bc)).astype(BF16))
            att = jnp.where(tril, att, 0.0)
            o = _dot(att.astype(BF16), vh)
            st = st_ref[hd]
            o = o + _dot_nt((qh * jnp.exp2(bc)).astype(BF16), st.astype(BF16))
            k_out = (kh * jnp.exp2(end - bc)).astype(BF16)
            st_ref[hd] = st * jnp.exp2(end) + _dot_tn(vh, k_out)
            o = _rms(o, nw_ref[...]) * _silu(rh)
            o_ref[r0:r0 + q, hd * hv:(hd + 1) * hv] = o.astype(o_ref.dtype)


def _gla_core(qkvr, g_low, w_gate2, b_gate, norm_w):
    t = qkvr.shape[0]
    rows = GLA_CHUNK * GLA_CHUNKS_PER_STEP
    return pl.pallas_call(
        _gla_core_kernel,
        grid=(t // rows,),
        in_specs=[
            pl.BlockSpec((rows, GLA_MAIN_DIM), lambda c: (c, 0)),
            pl.BlockSpec((rows, LANES), lambda c: (c, 0)),
            pl.BlockSpec((LANES, GLA_D_K), lambda c: (0, 0)),
            pl.BlockSpec((1, GLA_D_K), lambda c: (0, 0)),
            pl.BlockSpec((1, GLA_HEAD_V), lambda c: (0, 0)),
        ],
        out_specs=pl.BlockSpec((rows, GLA_D_V), lambda c: (c, 0)),
        out_shape=jax.ShapeDtypeStruct((t, GLA_D_V), BF16),
        scratch_shapes=[pltpu.VMEM((GLA_N_HEADS, GLA_HEAD_V, GLA_HEAD_K), F32)],
        compiler_params=_params("arbitrary"),
        name="gla_core",
    )(qkvr, g_low, w_gate2, b_gate, norm_w)


def _gla_mixer(h, mix_norm3, layer, j, gla_w_in, gla_w_gate2, gla_b_gate, gla_norm, gla_w_out):
    qkvr, g_low = _gla_in(h, mix_norm3, gla_w_in, layer, j)
    w_gate2 = jnp.pad(gla_w_gate2[j], ((0, LANES - GLA_GATE_RANK), (0, 0)))
    o = _gla_core(qkvr, g_low, w_gate2, gla_b_gate[j].reshape(1, -1), gla_norm[j].reshape(1, -1))
    return _mm_res(o, gla_w_out, j, h)


def _sgu_in_kernel(h_ref, g_ref, w_ref, b_ref, o_ref, n_ref):
    @pl.when(pl.program_id(1) == 0)
    def _():
        n_ref[...] = _rms(h_ref[...], g_ref[...]).astype(BF16)

    y = _dot(n_ref[...], w_ref[...].astype(BF16)) + b_ref[...]
    o_ref[...] = (0.5 * y * (1.0 + lax.erf(y * (0.5 ** 0.5)))).astype(o_ref.dtype)


def _sgu_in(h, mix_norm3, sgu_w_in, sgu_b_in3, layer, j):
    t = h.shape[0]
    return pl.pallas_call(
        _sgu_in_kernel,
        grid=(t // PROJ_TM, 2 * SGU_WIDTH // PROJ_TN),
        in_specs=[
            pl.BlockSpec((PROJ_TM, D_MODEL), lambda i, c: (i, 0), pipeline_mode=pl.Buffered(1)),
            pl.BlockSpec((None, 1, D_MODEL), lambda i, c: (layer, 0, 0)),
            pl.BlockSpec((None, D_MODEL, PROJ_TN), lambda i, c: (j, 0, c)),
            pl.BlockSpec((None, 1, PROJ_TN), lambda i, c: (j, 0, c)),
        ],
        out_specs=pl.BlockSpec((PROJ_TM, PROJ_TN), lambda i, c: (i, c)),
        out_shape=jax.ShapeDtypeStruct((t, 2 * SGU_WIDTH), BF16),
        scratch_shapes=[pltpu.VMEM((PROJ_TM, D_MODEL), BF16)],
        compiler_params=_params("parallel", "arbitrary"),
        name="sgu_in",
    )(h, mix_norm3, sgu_w_in, sgu_b_in3)


def _sgu_core_kernel(u_ref, v_ref, nw_ref, ws_ref, bs_ref, o_ref):
    q = SGU_CHUNK
    gd = SGU_GROUP_DIM
    row = lax.broadcasted_iota(jnp.int32, (q, q), 0)
    col = lax.broadcasted_iota(jnp.int32, (q, q), 1)
    tril = row >= col
    for ci in range(SGU_CHUNKS_PER_STEP):
        r0 = ci * q
        vn = _rms(v_ref[r0:r0 + q, :].astype(F32), nw_ref[...]).astype(BF16)
        for g in range(SGU_N_GROUPS):
            c0 = g * gd
            wc = jnp.where(tril, ws_ref[g], 0.0).astype(BF16)
            sv = _dot(wc, vn[:, c0:c0 + gd]) + bs_ref[:, g:g + 1]
            o_ref[r0:r0 + q, c0:c0 + gd] = (u_ref[r0:r0 + q, c0:c0 + gd].astype(F32) * sv).astype(o_ref.dtype)


def _sgu_core(zz, norm_w, w_s, b_s_t):
    t = zz.shape[0]
    rows = SGU_CHUNK * SGU_CHUNKS_PER_STEP
    return pl.pallas_call(
        _sgu_core_kernel,
        grid=(t // rows,),
        in_specs=[
            pl.BlockSpec((rows, SGU_WIDTH), lambda c: (c, 0)),
            pl.BlockSpec((rows, SGU_WIDTH), lambda c: (c, 1)),
            pl.BlockSpec((1, SGU_WIDTH), lambda c: (0, 0)),
            pl.BlockSpec((SGU_N_GROUPS, SGU_CHUNK, SGU_CHUNK), lambda c: (0, 0, 0)),
            pl.BlockSpec((SGU_CHUNK, SGU_N_GROUPS), lambda c: (0, 0)),
        ],
        out_specs=pl.BlockSpec((rows, SGU_WIDTH), lambda c: (c, 0)),
        out_shape=jax.ShapeDtypeStruct((t, SGU_WIDTH), BF16),
        compiler_params=_params("parallel"),
        name="sgu_core",
    )(zz, zz, norm_w, w_s, b_s_t)


def _sgu_mixer(h, mix_norm3, layer, j, sgu_w_in, sgu_b_in, sgu_norm, sgu_w_s, sgu_b_s, sgu_w_out):
    zz = _sgu_in(h, mix_norm3, sgu_w_in, sgu_b_in.reshape(sgu_b_in.shape[0], 1, -1), layer, j)
    y = _sgu_core(zz, sgu_norm[j].reshape(1, -1), sgu_w_s[j], sgu_b_s[j].T)
    return _mm_res(y, sgu_w_out, j, h)


def kernel(x, ffn_norm, ffn_w_in, ffn_w_out, mix_norm, ssd_w_in, ssd_conv_w, ssd_conv_b, ssd_dt_bias,
           ssd_a_log, ssd_d, ssd_norm, ssd_w_out, gla_w_in, gla_w_gate2, gla_b_gate, gla_norm, gla_w_out,
           sgu_w_in, sgu_b_in, sgu_norm, sgu_w_s, sgu_b_s, sgu_w_out, final_norm):
    b, t, d = x.shape
    assert (b, t, d) == (1, SEQ, D_MODEL)
    h = x.reshape(t, d)
    ffn_norm4 = ffn_norm.reshape(DEPTH, 2, 1, D_MODEL)
    mix_norm3 = mix_norm.reshape(DEPTH, 1, D_MODEL)
    final_w = final_norm.reshape(1, D_MODEL)
    for i in range(DEPTH):
        h = _ffn(h, ffn_norm4, ffn_w_in, ffn_w_out, final_w, i, 0, False)
        kind, j = i % N_MIXERS, i // N_MIXERS
        if kind == 0:
            h = _ssd_mixer(h, mix_norm3, i, j, ssd_w_in, ssd_conv_w, ssd_conv_b, ssd_dt_bias, ssd_a_log,
                           ssd_d, ssd_norm, ssd_w_out)
        elif kind == 1:
            h = _gla_mixer(h, mix_norm3, i, j, gla_w_in, gla_w_gate2, gla_b_gate, gla_norm, gla_w_out)
        else:
            h = _sgu_mixer(h, mix_norm3, i, j, sgu_w_in, sgu_b_in, sgu_norm, sgu_w_s, sgu_b_s, sgu_w_out)
        h = _ffn(h, ffn_norm4, ffn_w_in, ffn_w_out, final_w, i, 1, i == DEPTH - 1)
    return h.reshape(b, t, d)
```

```python
import functools

import jax
import jax.numpy as jnp
import numpy as np
from jax import lax
from jax.experimental import pallas as pl
from jax.experimental.pallas import tpu as pltpu

F32 = jnp.float32
BF16 = jnp.bfloat16

D_MODEL = 2048
SEQ = 8192
DEPTH = 4
N_MIXERS = 3
NORM_EPS = 1e-6
D_FF = 5632

SSD_D_INNER = 4096
SSD_HEAD_DIM = 64
SSD_N_HEADS = 64
SSD_N_GROUPS = 8
SSD_HEADS_PER_GROUP = 8
SSD_D_STATE = 128
SSD_CONV = 4
SSD_CHUNK = 128
SSD_CONV_DIM = 6144
SSD_GROUP_WIDTH = SSD_HEADS_PER_GROUP * SSD_HEAD_DIM
SSD_MAIN_DIM = SSD_D_INNER + SSD_CONV_DIM

GLA_N_HEADS = 4
GLA_D_K = 1024
GLA_D_V = 2048
GLA_HEAD_K = 256
GLA_HEAD_V = 512
GLA_GATE_RANK = 16
GLA_GATE_TAU = 16.0
GLA_CHUNK = 64
GLA_MAIN_DIM = 2 * GLA_D_K + 2 * GLA_D_V

SGU_WIDTH = 4096
SGU_N_GROUPS = 8
SGU_GROUP_DIM = 512
SGU_CHUNK = 128

FFN_TM = 1024
FFN_TF = 512
PROJ_TM = 1024
PROJ_TN = 1024
OUT_TM = 1024
OUT_TN = 512
GLA_CHUNKS_PER_STEP = 4
SGU_CHUNKS_PER_STEP = 2
LANES = 128
BF16_ROWS = 16
CONV_TAIL = BF16_ROWS
CONV_K = -(-(SSD_CONV - 1) * (SSD_CHUNK + CONV_TAIL) // LANES) * LANES
LOG2E = 1.4426950408889634

VMEM_LIMIT = 56 * 1024 * 1024
FFN_VMEM_LIMIT = 62 * 1024 * 1024


def _silu(x):
    hx = 0.5 * x
    return hx * jnp.tanh(hx) + hx


def _softplus(x):
    return jnp.maximum(x, 0.0) + jnp.log1p(jnp.exp(-jnp.abs(x)))


def _rms(x, w):
    return x * lax.rsqrt(jnp.mean(x * x, axis=-1, keepdims=True) + NORM_EPS) * w


def _dot(a, b):
    return jnp.dot(a, b, preferred_element_type=F32)


def _dot_nt(a, b):
    return lax.dot_general(a, b, (((1,), (1,)), ((), ())), preferred_element_type=F32)


def _dot_tn(a, b):
    return lax.dot_general(a, b, (((0,), (0,)), ((), ())), preferred_element_type=F32)


def _split3(x):
    x1 = x.astype(BF16)
    r1 = x - x1.astype(F32)
    x2 = r1.astype(BF16)
    r2 = r1 - x2.astype(F32)
    return x1, x2, r2.astype(BF16)


def _params(*sem):
    return pltpu.CompilerParams(dimension_semantics=sem, vmem_limit_bytes=VMEM_LIMIT)


def _ffn_kernel(h_ref, g_ref, wg_ref, wu_ref, wo_ref, fw_ref, o_ref, n_ref, *, final):
    j = pl.program_id(1)

    @pl.when(j == 0)
    def _():
        x = h_ref[...]
        n_ref[...] = _rms(x, g_ref[...]).astype(BF16)
        o_ref[...] = x

    n = n_ref[...]
    gate = _dot(n, wg_ref[...].astype(BF16))
    up = _dot(n, wu_ref[...].astype(BF16))
    act = (0.5 * _silu(gate) * up).astype(BF16)
    o_ref[...] += _dot(act, wo_ref[...].astype(BF16))

    if final:
        @pl.when(j == pl.num_programs(1) - 1)
        def _():
            o_ref[...] = _rms(o_ref[...], fw_ref[...])


def _ffn(h, ffn_norm4, ffn_w_in, ffn_w_out, final_w, layer, half, final):
    t = h.shape[0]
    nf = D_FF // FFN_TF
    return pl.pallas_call(
        functools.partial(_ffn_kernel, final=final),
        grid=(t // FFN_TM, nf),
        in_specs=[
            pl.BlockSpec((FFN_TM, D_MODEL), lambda i, j: (i, 0), pipeline_mode=pl.Buffered(1)),
            pl.BlockSpec((None, None, 1, D_MODEL), lambda i, j: (layer, half, 0, 0)),
            pl.BlockSpec((None, None, D_MODEL, FFN_TF), lambda i, j: (layer, half, 0, j)),
            pl.BlockSpec((None, None, D_MODEL, FFN_TF), lambda i, j: (layer, half, 0, j + nf)),
            pl.BlockSpec((None, None, FFN_TF, D_MODEL), lambda i, j: (layer, half, j, 0)),
            pl.BlockSpec((1, D_MODEL), lambda i, j: (0, 0)),
        ],
        out_specs=pl.BlockSpec((FFN_TM, D_MODEL), lambda i, j: (i, 0)),
        out_shape=jax.ShapeDtypeStruct((t, D_MODEL), F32),
        scratch_shapes=[pltpu.VMEM((FFN_TM, D_MODEL), BF16)],
        compiler_params=pltpu.CompilerParams(dimension_semantics=("parallel", "arbitrary"),
                                             vmem_limit_bytes=FFN_VMEM_LIMIT),
        name="ffn",
    )(h, ffn_norm4, ffn_w_in, ffn_w_in, ffn_w_out, final_w)


def _mm_res_kernel(a_ref, w_ref, r_ref, o_ref, wb_ref):
    @pl.when(pl.program_id(1) == 0)
    def _():
        wb_ref[...] = w_ref[...].astype(BF16)

    o_ref[...] = r_ref[...] + _dot(a_ref[...], wb_ref[...])


def _mm_res(a, w, layer, res):
    t, k = a.shape
    return pl.pallas_call(
        _mm_res_kernel,
        grid=(D_MODEL // OUT_TN, t // OUT_TM),
        in_specs=[
            pl.BlockSpec((OUT_TM, k), lambda j, i: (i, 0)),
            pl.BlockSpec((None, k, OUT_TN), lambda j, i: (layer, 0, j), pipeline_mode=pl.Buffered(1)),
            pl.BlockSpec((OUT_TM, OUT_TN), lambda j, i: (i, j)),
        ],
        out_specs=pl.BlockSpec((OUT_TM, OUT_TN), lambda j, i: (i, j)),
        out_shape=jax.ShapeDtypeStruct((t, D_MODEL), F32),
        scratch_shapes=[pltpu.VMEM((k, OUT_TN), BF16)],
        compiler_params=_params("parallel", "arbitrary"),
        name="out_proj",
    )(a, w, res)


def _narrow_rows(w_ref, valid):
    row = lax.broadcasted_iota(jnp.int32, (LANES, 1), 0)
    return jnp.where(row < valid, w_ref[...], 0.0).astype(BF16)


def _ssd_in_kernel(h_ref, g_ref, w_ref, wdt_ref, o_ref, dt_ref, n_ref):
    @pl.when(pl.program_id(1) == 0)
    def _():
        n = _rms(h_ref[...], g_ref[...]).astype(BF16)
        n_ref[...] = n
        dt_ref[...] = _dot_nt(n, _narrow_rows(wdt_ref, SSD_N_HEADS))

    o_ref[...] = _dot_nt(n_ref[...], w_ref[...].astype(BF16)).astype(o_ref.dtype)


def _ssd_in(h, mix_norm3, ssd_w_in_t, layer, j):
    t = h.shape[0]
    return pl.pallas_call(
        _ssd_in_kernel,
        grid=(t // PROJ_TM, SSD_MAIN_DIM // PROJ_TN),
        in_specs=[
            pl.BlockSpec((PROJ_TM, D_MODEL), lambda i, c: (i, 0), pipeline_mode=pl.Buffered(1)),
            pl.BlockSpec((None, 1, D_MODEL), lambda i, c: (layer, 0, 0)),
            pl.BlockSpec((None, PROJ_TN, D_MODEL), lambda i, c: (j, c, 0)),
            pl.BlockSpec((None, LANES, D_MODEL), lambda i, c: (j, SSD_MAIN_DIM // LANES, 0)),
        ],
        out_specs=[
            pl.BlockSpec((PROJ_TM, PROJ_TN), lambda i, c: (i, c)),
            pl.BlockSpec((PROJ_TM, LANES), lambda i, c: (i, 0)),
        ],
        out_shape=[
            jax.ShapeDtypeStruct((t, SSD_MAIN_DIM), BF16),
            jax.ShapeDtypeStruct((t, LANES), F32),
        ],
        scratch_shapes=[pltpu.VMEM((PROJ_TM, D_MODEL), BF16)],
        compiler_params=_params("parallel", "arbitrary"),
        name="ssd_in",
    )(h, mix_norm3, ssd_w_in_t, ssd_w_in_t)


def _conv_shift_matrix():
    q, blk = SSD_CHUNK, SSD_CHUNK + CONV_TAIL
    s = np.zeros((q, CONV_K), np.float32)
    for k in range(SSD_CONV - 1):
        shift = SSD_CONV - 1 - k
        for t in range(q):
            src = t - shift
            s[t, k * blk + (src if src >= 0 else q + CONV_TAIL + src)] = 1.0
    return jnp.asarray(s, BF16)


def _ssd_core_kernel(z_ref, xs_ref, bc_ref, dt_ref, cw_ref, cb_ref, dtb_ref, al_ref, dex_ref, nw_ref,
                     e_ref, e3_ref, sh_ref, o_ref, st_ref, tail_ref):
    q = SSD_CHUNK
    gw = SSD_GROUP_WIDTH
    ns = SSD_D_STATE

    @pl.when(pl.program_id(0) == 0)
    def _():
        st_ref[...] = jnp.zeros_like(st_ref)
        tail_ref[...] = jnp.zeros_like(tail_ref)

    def conv(src_ref, src_lo, lo, width):
        cur = src_ref[:, src_lo:src_lo + width]
        tail = tail_ref[:, lo:lo + width]
        parts = []
        for k in range(SSD_CONV - 1):
            wk = cw_ref[k:k + 1, lo:lo + width].astype(BF16)
            parts += [cur * wk, tail * wk]
        parts.append(jnp.zeros((CONV_K - (SSD_CONV - 1) * (q + CONV_TAIL), width), BF16))
        acc = _dot(sh_ref[...], jnp.concatenate(parts, axis=0))
        acc = acc + cur.astype(F32) * cw_ref[SSD_CONV - 1:SSD_CONV, lo:lo + width] + cb_ref[:, lo:lo + width]
        return _silu(acc)

    row = lax.broadcasted_iota(jnp.int32, (q, q), 0)
    col = lax.broadcasted_iota(jnp.int32, (q, q), 1)
    tril = row >= col
    lower = jnp.where(tril, 1.0, 0.0).astype(BF16)

    dt = _softplus(dt_ref[...] + dtb_ref[...])
    da = dt * (-LOG2E * jnp.exp(al_ref[...]))
    acum = sum(_dot(lower, p) for p in _split3(da))
    acum_t = acum.T
    acum3 = jnp.concatenate(_split3(acum), axis=1)
    dt16 = dt.astype(BF16)

    lane = lax.broadcasted_iota(jnp.int32, (q, 2 * SSD_HEAD_DIM), 1)
    even_head = lane < SSD_HEAD_DIM

    for g in range(SSD_N_GROUPS):
        c0 = g * gw
        ae = _dot(acum3, e3_ref[:, c0:c0 + gw])
        dte = _dot(dt16, e_ref[:, c0:c0 + gw])
        xs = conv(xs_ref, c0, c0, gw)
        b0 = g * ns
        c1 = SSD_N_GROUPS * ns + g * ns
        bg = conv(bc_ref, b0, SSD_D_INNER + b0, ns).astype(BF16)
        cg = conv(bc_ref, c1, SSD_D_INNER + c1, ns).astype(BF16)
        cbm = _dot_nt(cg, bg)
        xdt = xs * dte

        st = st_ref[g]
        y = _dot(cg, st.astype(BF16)) * jnp.exp2(ae)

        pieces = []
        for pr in range(SSD_HEADS_PER_GROUP // 2):
            def scores(hh):
                seg = acum[:, hh:hh + 1] - acum_t[hh:hh + 1, :]
                return (cbm * jnp.exp2(jnp.where(tril, seg, -jnp.inf))).astype(BF16)

            h0 = g * SSD_HEADS_PER_GROUP + 2 * pr
            lhs = jnp.concatenate([scores(h0), scores(h0 + 1)], axis=1)
            l0 = pr * 2 * SSD_HEAD_DIM
            xp = xdt[:, l0:l0 + 2 * SSD_HEAD_DIM]
            rhs = jnp.concatenate([jnp.where(even_head, xp, 0.0),
                                   jnp.where(even_head, 0.0, xp)], axis=0)
            pieces.append(_dot(lhs, rhs.astype(BF16)))
        y = y + jnp.concatenate(pieces, axis=1)

        last = ae[q - 1:q, :]
        to_end = jnp.exp2(last - ae)
        st_ref[g] = st * jnp.exp2(last) + _dot_tn(bg, (xdt * to_end).astype(BF16))

        y = y + dex_ref[:, c0:c0 + gw] * xs
        gated = y * _silu(z_ref[:, c0:c0 + gw].astype(F32))
        o_ref[:, c0:c0 + gw] = _rms(gated, nw_ref[:, c0:c0 + gw]).astype(o_ref.dtype)

    tail_ref[:, 0:SSD_D_INNER] = xs_ref[q - CONV_TAIL:q, :]
    tail_ref[:, SSD_D_INNER:SSD_CONV_DIM] = bc_ref[q - CONV_TAIL:q, :]


def _ssd_core(zx, dt, conv_w_t, conv_b, dt_bias, a_log, d_exp, norm_w, expand):
    t = zx.shape[0]
    q = SSD_CHUNK
    full = lambda shape: pl.BlockSpec(shape, lambda c: (0,) * len(shape))
    pad_heads = lambda v: jnp.pad(v, (0, LANES - SSD_N_HEADS)).reshape(1, LANES)
    return pl.pallas_call(
        _ssd_core_kernel,
        grid=(t // q,),
        in_specs=[
            pl.BlockSpec((q, SSD_D_INNER), lambda c: (c, 0)),
            pl.BlockSpec((q, SSD_D_INNER), lambda c: (c, 1)),
            pl.BlockSpec((q, 2 * SSD_N_GROUPS * SSD_D_STATE), lambda c: (c, 4)),
            pl.BlockSpec((q, LANES), lambda c: (c, 0)),
            full((SSD_CONV, SSD_CONV_DIM)),
            full((1, SSD_CONV_DIM)),
            full((1, LANES)),
            full((1, LANES)),
            full((1, SSD_D_INNER)),
            full((1, SSD_D_INNER)),
            full((LANES, SSD_D_INNER)),
            full((3 * LANES, SSD_D_INNER)),
            full((q, CONV_K)),
        ],
        out_specs=pl.BlockSpec((q, SSD_D_INNER), lambda c: (c, 0)),
        out_shape=jax.ShapeDtypeStruct((t, SSD_D_INNER), BF16),
        scratch_shapes=[
            pltpu.VMEM((SSD_N_GROUPS, SSD_D_STATE, SSD_GROUP_WIDTH), F32),
            pltpu.VMEM((CONV_TAIL, SSD_CONV_DIM), BF16),
        ],
        compiler_params=_params("arbitrary"),
        name="ssd_core",
    )(zx, zx, zx, dt, conv_w_t, conv_b, pad_heads(dt_bias), pad_heads(a_log), d_exp, norm_w,
      expand, jnp.tile(expand, (3, 1)), _conv_shift_matrix())


def _ssd_mixer(h, mix_norm3, layer, j, ssd_w_in, ssd_conv_w, ssd_conv_b, ssd_dt_bias, ssd_a_log,
               ssd_d, ssd_norm, ssd_w_out):
    zx, dt = _ssd_in(h, mix_norm3, jnp.swapaxes(ssd_w_in, 1, 2), layer, j)
    head_of_channel = jnp.arange(SSD_D_INNER) // SSD_HEAD_DIM
    expand = (jnp.arange(LANES)[:, None] == head_of_channel[None, :]).astype(BF16)
    d_exp = jnp.repeat(ssd_d[j], SSD_HEAD_DIM).reshape(1, SSD_D_INNER)
    yn = _ssd_core(zx, dt, ssd_conv_w[j].T, ssd_conv_b[j].reshape(1, -1), ssd_dt_bias[j],
                   ssd_a_log[j], d_exp, ssd_norm[j].reshape(1, -1), expand)
    return _mm_res(yn, ssd_w_out, j, h)


def _gla_in_kernel(h_ref, g_ref, w_ref, wl_ref, o_ref, gl_ref, n_ref):
    @pl.when(pl.program_id(1) == 0)
    def _():
        n = _rms(h_ref[...], g_ref[...]).astype(BF16)
        n_ref[...] = n
        gl_ref[...] = _dot_nt(n, _narrow_rows(wl_ref, GLA_GATE_RANK))

    o_ref[...] = _dot_nt(n_ref[...], w_ref[...].astype(BF16)).astype(o_ref.dtype)


def _gla_in(h, mix_norm3, gla_w_in_t, layer, j):
    t = h.shape[0]
    return pl.pallas_call(
        _gla_in_kernel,
        grid=(t // PROJ_TM, GLA_MAIN_DIM // PROJ_TN),
        in_specs=[
            pl.BlockSpec((PROJ_TM, D_MODEL), lambda i, c: (i, 0), pipeline_mode=pl.Buffered(1)),
            pl.BlockSpec((None, 1, D_MODEL), lambda i, c: (layer, 0, 0)),
            pl.BlockSpec((None, PROJ_TN, D_MODEL), lambda i, c: (j, c, 0)),
            pl.BlockSpec((None, LANES, D_MODEL), lambda i, c: (j, GLA_MAIN_DIM // LANES, 0)),
        ],
        out_specs=[
            pl.BlockSpec((PROJ_TM, PROJ_TN), lambda i, c: (i, c)),
            pl.BlockSpec((PROJ_TM, LANES), lambda i, c: (i, 0)),
        ],
        out_shape=[
            jax.ShapeDtypeStruct((t, GLA_MAIN_DIM), BF16),
            jax.ShapeDtypeStruct((t, LANES), F32),
        ],
        scratch_shapes=[pltpu.VMEM((PROJ_TM, D_MODEL), BF16)],
        compiler_params=_params("parallel", "arbitrary"),
        name="gla_in",
    )(h, mix_norm3, gla_w_in_t, gla_w_in_t)


def _gla_core_kernel(x_ref, gl_ref, w2_ref, bg_ref, nw_ref, o_ref, st_ref):
    q = GLA_CHUNK
    hk, hv = GLA_HEAD_K, GLA_HEAD_V

    @pl.when(pl.program_id(0) == 0)
    def _():
        st_ref[...] = jnp.zeros_like(st_ref)

    row = lax.broadcasted_iota(jnp.int32, (q, q), 0)
    col = lax.broadcasted_iota(jnp.int32, (q, q), 1)
    tril = row >= col
    lower = jnp.where(tril, 1.0, 0.0).astype(BF16)
    w2 = w2_ref[...].astype(BF16)

    for ci in range(GLA_CHUNKS_PER_STEP):
        r0 = ci * q
        lg = _dot(gl_ref[r0:r0 + q, :].astype(BF16), w2) + bg_ref[...]
        log_a = (jnp.minimum(lg, 0.0) - jnp.log1p(jnp.exp(-jnp.abs(lg)))) * (LOG2E / GLA_GATE_TAU)
        bcum = sum(_dot(lower, p) for p in _split3(log_a))
        for hd in range(GLA_N_HEADS):
            k0 = hd * hk
            v0 = 2 * GLA_D_K + hd * hv
            qh = x_ref[r0:r0 + q, k0:k0 + hk].astype(F32) * (hk ** -0.5)
            kh = x_ref[r0:r0 + q, GLA_D_K + k0:GLA_D_K + k0 + hk].astype(F32)
            vh = x_ref[r0:r0 + q, v0:v0 + hv]
            rh = x_ref[r0:r0 + q, v0 + GLA_D_V:v0 + GLA_D_V + hv].astype(F32)
            bc = bcum[:, k0:k0 + hk]
            mid = bc[q // 2:q // 2 + 1, :]
            end = bc[q - 1:q, :]
            att = _dot_nt((qh * jnp.exp2(bc - mid)).astype(BF16), (kh * jnp.exp2(mid - bc)).astype(BF16))
            att = jnp.where(tril, att, 0.0)
            o = _dot(att.astype(BF16), vh)
            st = st_ref[hd]
            o = o + _dot_nt((qh * jnp.exp2(bc)).astype(BF16), st.astype(BF16))
            k_out = (kh * jnp.exp2(end - bc)).astype(BF16)
            st_ref[hd] = st * jnp.exp2(end) + _dot_tn(vh, k_out)
            o = _rms(o, nw_ref[...]) * _silu(rh)
            o_ref[r0:r0 + q, hd * hv:(hd + 1) * hv] = o.astype(o_ref.dtype)


def _gla_core(qkvr, g_low, w_gate2, b_gate, norm_w):
    t = qkvr.shape[0]
    rows = GLA_CHUNK * GLA_CHUNKS_PER_STEP
    return pl.pallas_call(
        _gla_core_kernel,
        grid=(t // rows,),
        in_specs=[
            pl.BlockSpec((rows, GLA_MAIN_DIM), lambda c: (c, 0)),
            pl.BlockSpec((rows, LANES), lambda c: (c, 0)),
            pl.BlockSpec((LANES, GLA_D_K), lambda c: (0, 0)),
            pl.BlockSpec((1, GLA_D_K), lambda c: (0, 0)),
            pl.BlockSpec((1, GLA_HEAD_V), lambda c: (0, 0)),
        ],
        out_specs=pl.BlockSpec((rows, GLA_D_V), lambda c: (c, 0)),
        out_shape=jax.ShapeDtypeStruct((t, GLA_D_V), BF16),
        scratch_shapes=[pltpu.VMEM((GLA_N_HEADS, GLA_HEAD_V, GLA_HEAD_K), F32)],
        compiler_params=_params("arbitrary"),
        name="gla_core",
    )(qkvr, g_low, w_gate2, b_gate, norm_w)


def _gla_mixer(h, mix_norm3, layer, j, gla_w_in, gla_w_gate2, gla_b_gate, gla_norm, gla_w_out):
    qkvr, g_low = _gla_in(h, mix_norm3, jnp.swapaxes(gla_w_in, 1, 2), layer, j)
    w_gate2 = jnp.pad(gla_w_gate2[j], ((0, LANES - GLA_GATE_RANK), (0, 0)))
    o = _gla_core(qkvr, g_low, w_gate2, gla_b_gate[j].reshape(1, -1), gla_norm[j].reshape(1, -1))
    return _mm_res(o, gla_w_out, j, h)


def _sgu_in_kernel(h_ref, g_ref, w_ref, b_ref, o_ref, n_ref):
    @pl.when(pl.program_id(1) == 0)
    def _():
        n_ref[...] = _rms(h_ref[...], g_ref[...]).astype(BF16)

    y = _dot(n_ref[...], w_ref[...].astype(BF16)) + b_ref[...]
    o_ref[...] = (0.5 * y * (1.0 + lax.erf(y * (0.5 ** 0.5)))).astype(o_ref.dtype)


def _sgu_in(h, mix_norm3, sgu_w_in, sgu_b_in3, layer, j):
    t = h.shape[0]
    return pl.pallas_call(
        _sgu_in_kernel,
        grid=(t // PROJ_TM, 2 * SGU_WIDTH // PROJ_TN),
        in_specs=[
            pl.BlockSpec((PROJ_TM, D_MODEL), lambda i, c: (i, 0), pipeline_mode=pl.Buffered(1)),
            pl.BlockSpec((None, 1, D_MODEL), lambda i, c: (layer, 0, 0)),
            pl.BlockSpec((None, D_MODEL, PROJ_TN), lambda i, c: (j, 0, c)),
            pl.BlockSpec((None, 1, PROJ_TN), lambda i, c: (j, 0, c)),
        ],
        out_specs=pl.BlockSpec((PROJ_TM, PROJ_TN), lambda i, c: (i, c)),
        out_shape=jax.ShapeDtypeStruct((t, 2 * SGU_WIDTH), BF16),
        scratch_shapes=[pltpu.VMEM((PROJ_TM, D_MODEL), BF16)],
        compiler_params=_params("parallel", "arbitrary"),
        name="sgu_in",
    )(h, mix_norm3, sgu_w_in, sgu_b_in3)


def _sgu_core_kernel(u_ref, v_ref, nw_ref, ws_ref, bs_ref, o_ref):
    q = SGU_CHUNK
    gd = SGU_GROUP_DIM
    row = lax.broadcasted_iota(jnp.int32, (q, q), 0)
    col = lax.broadcasted_iota(jnp.int32, (q, q), 1)
    tril = row >= col
    for ci in range(SGU_CHUNKS_PER_STEP):
        r0 = ci * q
        vn = _rms(v_ref[r0:r0 + q, :].astype(F32), nw_ref[...]).astype(BF16)
        for g in range(SGU_N_GROUPS):
            c0 = g * gd
            wc = jnp.where(tril, ws_ref[g], 0.0).astype(BF16)
            sv = _dot(wc, vn[:, c0:c0 + gd]) + bs_ref[:, g:g + 1]
            o_ref[r0:r0 + q, c0:c0 + gd] = (u_ref[r0:r0 + q, c0:c0 + gd].astype(F32) * sv).astype(o_ref.dtype)


def _sgu_core(zz, norm_w, w_s, b_s_t):
    t = zz.shape[0]
    rows = SGU_CHUNK * SGU_CHUNKS_PER_STEP
    return pl.pallas_call(
        _sgu_core_kernel,
        grid=(t // rows,),
        in_specs=[
            pl.BlockSpec((rows, SGU_WIDTH), lambda c: (c, 0)),
            pl.BlockSpec((rows, SGU_WIDTH), lambda c: (c, 1)),
            pl.BlockSpec((1, SGU_WIDTH), lambda c: (0, 0)),
            pl.BlockSpec((SGU_N_GROUPS, SGU_CHUNK, SGU_CHUNK), lambda c: (0, 0, 0)),
            pl.BlockSpec((SGU_CHUNK, SGU_N_GROUPS), lambda c: (0, 0)),
        ],
        out_specs=pl.BlockSpec((rows, SGU_WIDTH), lambda c: (c, 0)),
        out_shape=jax.ShapeDtypeStruct((t, SGU_WIDTH), BF16),
        compiler_params=_params("parallel"),
        name="sgu_core",
    )(zz, zz, norm_w, w_s, b_s_t)


def _sgu_mixer(h, mix_norm3, layer, j, sgu_w_in, sgu_b_in, sgu_norm, sgu_w_s, sgu_b_s, sgu_w_out):
    zz = _sgu_in(h, mix_norm3, sgu_w_in, sgu_b_in.reshape(sgu_b_in.shape[0], 1, -1), layer, j)
    y = _sgu_core(zz, sgu_norm[j].reshape(1, -1), sgu_w_s[j], sgu_b_s[j].T)
    return _mm_res(y, sgu_w_out, j, h)


def kernel(x, ffn_norm, ffn_w_in, ffn_w_out, mix_norm, ssd_w_in, ssd_conv_w, ssd_conv_b, ssd_dt_bias,
           ssd_a_log, ssd_d, ssd_norm, ssd_w_out, gla_w_in, gla_w_gate2, gla_b_gate, gla_norm, gla_w_out,
           sgu_w_in, sgu_b_in, sgu_norm, sgu_w_s, sgu_b_s, sgu_w_out, final_norm):
    b, t, d = x.shape
    assert (b, t, d) == (1, SEQ, D_MODEL)
    h = x.reshape(t, d)
    ffn_norm4 = ffn_norm.reshape(DEPTH, 2, 1, D_MODEL)
    mix_norm3 = mix_norm.reshape(DEPTH, 1, D_MODEL)
    final_w = final_norm.reshape(1, D_MODEL)
    for i in range(DEPTH):
        h = _ffn(h, ffn_norm4, ffn_w_in, ffn_w_out, final_w, i, 0, False)
        kind, j = i % N_MIXERS, i // N_MIXERS
        if kind == 0:
            h = _ssd_mixer(h, mix_norm3, i, j, ssd_w_in, ssd_conv_w, ssd_conv_b, ssd_dt_bias, ssd_a_log,
                           ssd_d, ssd_norm, ssd_w_out)
        elif kind == 1:
            h = _gla_mixer(h, mix_norm3, i, j, gla_w_in, gla_w_gate2, gla_b_gate, gla_norm, gla_w_out)
        else:
            h = _sgu_mixer(h, mix_norm3, i, j, sgu_w_in, sgu_b_in, sgu_norm, sgu_w_s, sgu_b_s, sgu_w_out)
        h = _ffn(h, ffn_norm4, ffn_w_in, ffn_w_out, final_w, i, 1, i == DEPTH - 1)
    return h.reshape(b, t, d)
```

```python
import functools

import jax
import jax.numpy as jnp
import numpy as np
from jax import lax
from jax.experimental import pallas as pl
from jax.experimental.pallas import tpu as pltpu

F32 = jnp.float32
BF16 = jnp.bfloat16

D_MODEL = 2048
SEQ = 8192
DEPTH = 4
N_MIXERS = 3
NORM_EPS = 1e-6
D_FF = 5632

SSD_D_INNER = 4096
SSD_HEAD_DIM = 64
SSD_N_HEADS = 64
SSD_N_GROUPS = 8
SSD_HEADS_PER_GROUP = 8
SSD_D_STATE = 128
SSD_CONV = 4
SSD_CHUNK = 128
SSD_CONV_DIM = 6144
SSD_GROUP_WIDTH = SSD_HEADS_PER_GROUP * SSD_HEAD_DIM
SSD_MAIN_DIM = SSD_D_INNER + SSD_CONV_DIM

GLA_N_HEADS = 4
GLA_D_K = 1024
GLA_D_V = 2048
GLA_HEAD_K = 256
GLA_HEAD_V = 512
GLA_GATE_RANK = 16
GLA_GATE_TAU = 16.0
GLA_CHUNK = 64
GLA_MAIN_DIM = 2 * GLA_D_K + 2 * GLA_D_V

SGU_WIDTH = 4096
SGU_N_GROUPS = 8
SGU_GROUP_DIM = 512
SGU_CHUNK = 128

FFN_TM = 1024
FFN_TF = 512
PROJ_TM = 1024
PROJ_TN = 1024
OUT_TM = 512
OUT_TN = 1024
SSD_CHUNKS_PER_STEP = 2
GLA_CHUNKS_PER_STEP = 4
SGU_CHUNKS_PER_STEP = 4
LANES = 128
BF16_ROWS = 16
CONV_TAIL = BF16_ROWS
CONV_K = -(-(SSD_CONV - 1) * (SSD_CHUNK + CONV_TAIL) // LANES) * LANES
LOG2E = 1.4426950408889634

VMEM_LIMIT = 56 * 1024 * 1024
FFN_VMEM_LIMIT = 62 * 1024 * 1024


def _silu(x):
    hx = 0.5 * x
    return hx * jnp.tanh(hx) + hx


def _softplus(x):
    return jnp.maximum(x, 0.0) + jnp.log1p(jnp.exp(-jnp.abs(x)))


def _rms(x, w):
    return x * lax.rsqrt(jnp.mean(x * x, axis=-1, keepdims=True) + NORM_EPS) * w


def _dot(a, b):
    return jnp.dot(a, b, preferred_element_type=F32)


def _dot_nt(a, b):
    return lax.dot_general(a, b, (((1,), (1,)), ((), ())), preferred_element_type=F32)


def _dot_tn(a, b):
    return lax.dot_general(a, b, (((0,), (0,)), ((), ())), preferred_element_type=F32)


def _split3(x):
    x1 = x.astype(BF16)
    r1 = x - x1.astype(F32)
    x2 = r1.astype(BF16)
    r2 = r1 - x2.astype(F32)
    return x1, x2, r2.astype(BF16)


def _params(*sem):
    return pltpu.CompilerParams(dimension_semantics=sem, vmem_limit_bytes=VMEM_LIMIT)


def _ffn_kernel(h_ref, g_ref, wg_ref, wu_ref, wo_ref, fw_ref, o_ref, n_ref, *, final):
    j = pl.program_id(1)

    @pl.when(j == 0)
    def _():
        x = h_ref[...]
        n_ref[...] = _rms(x, g_ref[...]).astype(BF16)
        o_ref[...] = x

    n = n_ref[...]
    gate = _dot(n, wg_ref[...].astype(BF16))
    up = _dot(n, wu_ref[...].astype(BF16))
    act = (0.5 * _silu(gate) * up).astype(BF16)
    o_ref[...] += _dot(act, wo_ref[...].astype(BF16))

    if final:
        @pl.when(j == pl.num_programs(1) - 1)
        def _():
            o_ref[...] = _rms(o_ref[...], fw_ref[...])


def _ffn(h, ffn_norm4, ffn_w_in, ffn_w_out, final_w, layer, half, final):
    t = h.shape[0]
    nf = D_FF // FFN_TF
    return pl.pallas_call(
        functools.partial(_ffn_kernel, final=final),
        grid=(t // FFN_TM, nf),
        in_specs=[
            pl.BlockSpec((FFN_TM, D_MODEL), lambda i, j: (i, 0), pipeline_mode=pl.Buffered(1)),
            pl.BlockSpec((None, None, 1, D_MODEL), lambda i, j: (layer, half, 0, 0)),
            pl.BlockSpec((None, None, D_MODEL, FFN_TF), lambda i, j: (layer, half, 0, j)),
            pl.BlockSpec((None, None, D_MODEL, FFN_TF), lambda i, j: (layer, half, 0, j + nf)),
            pl.BlockSpec((None, None, FFN_TF, D_MODEL), lambda i, j: (layer, half, j, 0)),
            pl.BlockSpec((1, D_MODEL), lambda i, j: (0, 0)),
        ],
        out_specs=pl.BlockSpec((FFN_TM, D_MODEL), lambda i, j: (i, 0)),
        out_shape=jax.ShapeDtypeStruct((t, D_MODEL), F32),
        scratch_shapes=[pltpu.VMEM((FFN_TM, D_MODEL), BF16)],
        compiler_params=pltpu.CompilerParams(dimension_semantics=("parallel", "arbitrary"),
                                             vmem_limit_bytes=FFN_VMEM_LIMIT),
        name="ffn",
    )(h, ffn_norm4, ffn_w_in, ffn_w_in, ffn_w_out, final_w)


def _mm_res_kernel(a_ref, w_ref, r_ref, o_ref, wb_ref):
    @pl.when(pl.program_id(1) == 0)
    def _():
        wb_ref[...] = w_ref[...].astype(BF16)

    o_ref[...] = r_ref[...] + _dot(a_ref[...], wb_ref[...])


def _mm_res(a, w, layer, res):
    t, k = a.shape
    return pl.pallas_call(
        _mm_res_kernel,
        grid=(D_MODEL // OUT_TN, t // OUT_TM),
        in_specs=[
            pl.BlockSpec((OUT_TM, k), lambda j, i: (i, 0)),
            pl.BlockSpec((None, k, OUT_TN), lambda j, i: (layer, 0, j), pipeline_mode=pl.Buffered(1)),
            pl.BlockSpec((OUT_TM, OUT_TN), lambda j, i: (i, j)),
        ],
        out_specs=pl.BlockSpec((OUT_TM, OUT_TN), lambda j, i: (i, j)),
        out_shape=jax.ShapeDtypeStruct((t, D_MODEL), F32),
        scratch_shapes=[pltpu.VMEM((k, OUT_TN), BF16)],
        compiler_params=_params("parallel", "arbitrary"),
        name="out_proj",
    )(a, w, res)


def _narrow_rows(w_ref, valid):
    row = lax.broadcasted_iota(jnp.int32, (LANES, 1), 0)
    return jnp.where(row < valid, w_ref[...], 0.0).astype(BF16)


def _ssd_in_kernel(h_ref, g_ref, w_ref, wdt_ref, o_ref, dt_ref, n_ref):
    @pl.when(pl.program_id(1) == 0)
    def _():
        n = _rms(h_ref[...], g_ref[...]).astype(BF16)
        n_ref[...] = n
        dt_ref[...] = _dot_nt(n, _narrow_rows(wdt_ref, SSD_N_HEADS))

    o_ref[...] = _dot_nt(n_ref[...], w_ref[...].astype(BF16)).astype(o_ref.dtype)


def _ssd_in(h, mix_norm3, ssd_w_in_t, layer, j):
    t = h.shape[0]
    return pl.pallas_call(
        _ssd_in_kernel,
        grid=(t // PROJ_TM, SSD_MAIN_DIM // PROJ_TN),
        in_specs=[
            pl.BlockSpec((PROJ_TM, D_MODEL), lambda i, c: (i, 0), pipeline_mode=pl.Buffered(1)),
            pl.BlockSpec((None, 1, D_MODEL), lambda i, c: (layer, 0, 0)),
            pl.BlockSpec((None, PROJ_TN, D_MODEL), lambda i, c: (j, c, 0)),
            pl.BlockSpec((None, LANES, D_MODEL), lambda i, c: (j, SSD_MAIN_DIM // LANES, 0)),
        ],
        out_specs=[
            pl.BlockSpec((PROJ_TM, PROJ_TN), lambda i, c: (i, c)),
            pl.BlockSpec((PROJ_TM, LANES), lambda i, c: (i, 0)),
        ],
        out_shape=[
            jax.ShapeDtypeStruct((t, SSD_MAIN_DIM), BF16),
            jax.ShapeDtypeStruct((t, LANES), F32),
        ],
        scratch_shapes=[pltpu.VMEM((PROJ_TM, D_MODEL), BF16)],
        compiler_params=_params("parallel", "arbitrary"),
        name="ssd_in",
    )(h, mix_norm3, ssd_w_in_t, ssd_w_in_t)


def _conv_shift_matrix():
    q, blk = SSD_CHUNK, SSD_CHUNK + CONV_TAIL
    s = np.zeros((q, CONV_K), np.float32)
    for k in range(SSD_CONV - 1):
        shift = SSD_CONV - 1 - k
        for t in range(q):
            src = t - shift
            s[t, k * blk + (src if src >= 0 else q + CONV_TAIL + src)] = 1.0
    return jnp.asarray(s, BF16)


def _ssd_core_kernel(z_ref, xs_ref, bc_ref, dt_ref, cw_ref, cb_ref, dtb_ref, al_ref, dex_ref, nw_ref,
                     e_ref, e3_ref, sh_ref, o_ref, st_ref, tail_ref):
    q = SSD_CHUNK
    gw = SSD_GROUP_WIDTH
    ns = SSD_D_STATE

    @pl.when(pl.program_id(0) == 0)
    def _():
        st_ref[...] = jnp.zeros_like(st_ref)
        tail_ref[...] = jnp.zeros_like(tail_ref)

    row = lax.broadcasted_iota(jnp.int32, (q, q), 0)
    col = lax.broadcasted_iota(jnp.int32, (q, q), 1)
    tril = row >= col
    lower = jnp.where(tril, 1.0, 0.0).astype(BF16)
    lane = lax.broadcasted_iota(jnp.int32, (q, 2 * SSD_HEAD_DIM), 1)
    even_head = lane < SSD_HEAD_DIM

    for ci in range(SSD_CHUNKS_PER_STEP):
        r0 = ci * q

        def conv(src_ref, src_lo, lo, width):
            cur = src_ref[r0:r0 + q, src_lo:src_lo + width]
            if ci == 0:
                tail = tail_ref[:, lo:lo + width]
            else:
                tail = src_ref[r0 - CONV_TAIL:r0, src_lo:src_lo + width]
            parts = []
            for k in range(SSD_CONV - 1):
                wk = cw_ref[k:k + 1, lo:lo + width].astype(BF16)
                parts += [cur * wk, tail * wk]
            parts.append(jnp.zeros((CONV_K - (SSD_CONV - 1) * (q + CONV_TAIL), width), BF16))
            acc = _dot(sh_ref[...], jnp.concatenate(parts, axis=0))
            acc = acc + cur.astype(F32) * cw_ref[SSD_CONV - 1:SSD_CONV, lo:lo + width] + cb_ref[:, lo:lo + width]
            return _silu(acc)

        dt = _softplus(dt_ref[r0:r0 + q, :] + dtb_ref[...])
        da = dt * (-LOG2E * jnp.exp(al_ref[...]))
        acum = sum(_dot(lower, p) for p in _split3(da))
        acum_t = acum.T
        acum3 = jnp.concatenate(_split3(acum), axis=1)
        dt16 = dt.astype(BF16)

        for g in range(SSD_N_GROUPS):
            c0 = g * gw
            ae = _dot(acum3, e3_ref[:, c0:c0 + gw])
            dte = _dot(dt16, e_ref[:, c0:c0 + gw])
            xs = conv(xs_ref, c0, c0, gw)
            b0 = g * ns
            c1 = SSD_N_GROUPS * ns + g * ns
            bg = conv(bc_ref, b0, SSD_D_INNER + b0, ns).astype(BF16)
            cg = conv(bc_ref, c1, SSD_D_INNER + c1, ns).astype(BF16)
            cbm = _dot_nt(cg, bg)
            xdt = xs * dte

            st = st_ref[g]
            y = _dot(cg, st.astype(BF16)) * jnp.exp2(ae)

            pieces = []
            for pr in range(SSD_HEADS_PER_GROUP // 2):
                def scores(hh):
                    seg = acum[:, hh:hh + 1] - acum_t[hh:hh + 1, :]
                    return (cbm * jnp.exp2(jnp.where(tril, seg, -jnp.inf))).astype(BF16)

                h0 = g * SSD_HEADS_PER_GROUP + 2 * pr
                lhs = jnp.concatenate([scores(h0), scores(h0 + 1)], axis=1)
                l0 = pr * 2 * SSD_HEAD_DIM
                xp = xdt[:, l0:l0 + 2 * SSD_HEAD_DIM]
                rhs = jnp.concatenate([jnp.where(even_head, xp, 0.0),
                                       jnp.where(even_head, 0.0, xp)], axis=0)
                pieces.append(_dot(lhs, rhs.astype(BF16)))
            y = y + jnp.concatenate(pieces, axis=1)

            last = ae[q - 1:q, :]
            to_end = jnp.exp2(last - ae)
            st_ref[g] = st * jnp.exp2(last) + _dot_tn(bg, (xdt * to_end).astype(BF16))

            y = y + dex_ref[:, c0:c0 + gw] * xs
            gated = y * _silu(z_ref[r0:r0 + q, c0:c0 + gw].astype(F32))
            o_ref[r0:r0 + q, c0:c0 + gw] = _rms(gated, nw_ref[:, c0:c0 + gw]).astype(o_ref.dtype)

    rows = SSD_CHUNKS_PER_STEP * q
    tail_ref[:, 0:SSD_D_INNER] = xs_ref[rows - CONV_TAIL:rows, :]
    tail_ref[:, SSD_D_INNER:SSD_CONV_DIM] = bc_ref[rows - CONV_TAIL:rows, :]


def _ssd_core(zx, dt, conv_w_t, conv_b, dt_bias, a_log, d_exp, norm_w, expand):
    t = zx.shape[0]
    q = SSD_CHUNK
    rows = SSD_CHUNKS_PER_STEP * q
    full = lambda shape: pl.BlockSpec(shape, lambda c: (0,) * len(shape))
    pad_heads = lambda v: jnp.pad(v, (0, LANES - SSD_N_HEADS)).reshape(1, LANES)
    return pl.pallas_call(
        _ssd_core_kernel,
        grid=(t // rows,),
        in_specs=[
            pl.BlockSpec((rows, SSD_D_INNER), lambda c: (c, 0)),
            pl.BlockSpec((rows, SSD_D_INNER), lambda c: (c, 1)),
            pl.BlockSpec((rows, 2 * SSD_N_GROUPS * SSD_D_STATE), lambda c: (c, 4)),
            pl.BlockSpec((rows, LANES), lambda c: (c, 0)),
            full((SSD_CONV, SSD_CONV_DIM)),
            full((1, SSD_CONV_DIM)),
            full((1, LANES)),
            full((1, LANES)),
            full((1, SSD_D_INNER)),
            full((1, SSD_D_INNER)),
            full((LANES, SSD_D_INNER)),
            full((3 * LANES, SSD_D_INNER)),
            full((q, CONV_K)),
        ],
        out_specs=pl.BlockSpec((rows, SSD_D_INNER), lambda c: (c, 0)),
        out_shape=jax.ShapeDtypeStruct((t, SSD_D_INNER), BF16),
        scratch_shapes=[
            pltpu.VMEM((SSD_N_GROUPS, SSD_D_STATE, SSD_GROUP_WIDTH), F32),
            pltpu.VMEM((CONV_TAIL, SSD_CONV_DIM), BF16),
        ],
        compiler_params=_params("arbitrary"),
        name="ssd_core",
    )(zx, zx, zx, dt, conv_w_t, conv_b, pad_heads(dt_bias), pad_heads(a_log), d_exp, norm_w,
      expand, jnp.tile(expand, (3, 1)), _conv_shift_matrix())


def _ssd_mixer(h, mix_norm3, layer, j, ssd_w_in, ssd_conv_w, ssd_conv_b, ssd_dt_bias, ssd_a_log,
               ssd_d, ssd_norm, ssd_w_out):
    zx, dt = _ssd_in(h, mix_norm3, jnp.swapaxes(ssd_w_in, 1, 2), layer, j)
    head_of_channel = jnp.arange(SSD_D_INNER) // SSD_HEAD_DIM
    expand = (jnp.arange(LANES)[:, None] == head_of_channel[None, :]).astype(BF16)
    d_exp = jnp.repeat(ssd_d[j], SSD_HEAD_DIM).reshape(1, SSD_D_INNER)
    yn = _ssd_core(zx, dt, ssd_conv_w[j].T, ssd_conv_b[j].reshape(1, -1), ssd_dt_bias[j],
                   ssd_a_log[j], d_exp, ssd_norm[j].reshape(1, -1), expand)
    return _mm_res(yn, ssd_w_out, j, h)


def _gla_in_kernel(h_ref, g_ref, w_ref, wl_ref, o_ref, gl_ref, n_ref):
    @pl.when(pl.program_id(1) == 0)
    def _():
        n = _rms(h_ref[...], g_ref[...]).astype(BF16)
        n_ref[...] = n
        gl_ref[...] = _dot_nt(n, _narrow_rows(wl_ref, GLA_GATE_RANK))

    o_ref[...] = _dot_nt(n_ref[...], w_ref[...].astype(BF16)).astype(o_ref.dtype)


def _gla_in(h, mix_norm3, gla_w_in_t, layer, j):
    t = h.shape[0]
    return pl.pallas_call(
        _gla_in_kernel,
        grid=(t // PROJ_TM, GLA_MAIN_DIM // PROJ_TN),
        in_specs=[
            pl.BlockSpec((PROJ_TM, D_MODEL), lambda i, c: (i, 0), pipeline_mode=pl.Buffered(1)),
            pl.BlockSpec((None, 1, D_MODEL), lambda i, c: (layer, 0, 0)),
            pl.BlockSpec((None, PROJ_TN, D_MODEL), lambda i, c: (j, c, 0)),
            pl.BlockSpec((None, LANES, D_MODEL), lambda i, c: (j, GLA_MAIN_DIM // LANES, 0)),
        ],
        out_specs=[
            pl.BlockSpec((PROJ_TM, PROJ_TN), lambda i, c: (i, c)),
            pl.BlockSpec((PROJ_TM, LANES), lambda i, c: (i, 0)),
        ],
        out_shape=[
            jax.ShapeDtypeStruct((t, GLA_MAIN_DIM), BF16),
            jax.ShapeDtypeStruct((t, LANES), F32),
        ],
        scratch_shapes=[pltpu.VMEM((PROJ_TM, D_MODEL), BF16)],
        compiler_params=_params("parallel", "arbitrary"),
        name="gla_in",
    )(h, mix_norm3, gla_w_in_t, gla_w_in_t)


def _gla_core_kernel(x_ref, gl_ref, w2_ref, bg_ref, nw_ref, o_ref, st_ref):
    q = GLA_CHUNK
    hk, hv = GLA_HEAD_K, GLA_HEAD_V

    @pl.when(pl.program_id(0) == 0)
    def _():
        st_ref[...] = jnp.zeros_like(st_ref)

    row = lax.broadcasted_iota(jnp.int32, (q, q), 0)
    col = lax.broadcasted_iota(jnp.int32, (q, q), 1)
    tril = row >= col
    lower = jnp.where(tril, 1.0, 0.0).astype(BF16)
    w2 = w2_ref[...].astype(BF16)

    for ci in range(GLA_CHUNKS_PER_STEP):
        r0 = ci * q
        lg = _dot(gl_ref[r0:r0 + q, :].astype(BF16), w2) + bg_ref[...]
        log_a = (jnp.minimum(lg, 0.0) - jnp.log1p(jnp.exp(-jnp.abs(lg)))) * (LOG2E / GLA_GATE_TAU)
        bcum = sum(_dot(lower, p) for p in _split3(log_a))
        for hd in range(GLA_N_HEADS):
            k0 = hd * hk
            v0 = 2 * GLA_D_K + hd * hv
            qh = x_ref[r0:r0 + q, k0:k0 + hk].astype(F32) * (hk ** -0.5)
            kh = x_ref[r0:r0 + q, GLA_D_K + k0:GLA_D_K + k0 + hk].astype(F32)
            vh = x_ref[r0:r0 + q, v0:v0 + hv]
            rh = x_ref[r0:r0 + q, v0 + GLA_D_V:v0 + GLA_D_V + hv].astype(F32)
            bc = bcum[:, k0:k0 + hk]
            mid = bc[q // 2:q // 2 + 1, :]
            end = bc[q - 1:q, :]
            att = _dot_nt((qh * jnp.exp2(bc - mid)).astype(BF16), (kh * jnp.exp2(mid - bc)).astype(BF16))
            att = jnp.where(tril, att, 0.0)
            o = _dot(att.astype(BF16), vh)
            st = st_ref[hd]
            o = o + _dot_nt((qh * jnp.exp2(bc)).astype(BF16), st.astype(BF16))
            k_out = (kh * jnp.exp2(end - bc)).astype(BF16)
            st_ref[hd] = st * jnp.exp2(end) + _dot_tn(vh, k_out)
            o = _rms(o, nw_ref[...]) * _silu(rh)
            o_ref[r0:r0 + q, hd * hv:(hd + 1) * hv] = o.astype(o_ref.dtype)


def _gla_core(qkvr, g_low, w_gate2, b_gate, norm_w):
    t = qkvr.shape[0]
    rows = GLA_CHUNK * GLA_CHUNKS_PER_STEP
    return pl.pallas_call(
        _gla_core_kernel,
        grid=(t // rows,),
        in_specs=[
            pl.BlockSpec((rows, GLA_MAIN_DIM), lambda c: (c, 0)),
            pl.BlockSpec((rows, LANES), lambda c: (c, 0)),
            pl.BlockSpec((LANES, GLA_D_K), lambda c: (0, 0)),
            pl.BlockSpec((1, GLA_D_K), lambda c: (0, 0)),
            pl.BlockSpec((1, GLA_HEAD_V), lambda c: (0, 0)),
        ],
        out_specs=pl.BlockSpec((rows, GLA_D_V), lambda c: (c, 0)),
        out_shape=jax.ShapeDtypeStruct((t, GLA_D_V), BF16),
        scratch_shapes=[pltpu.VMEM((GLA_N_HEADS, GLA_HEAD_V, GLA_HEAD_K), F32)],
        compiler_params=_params("arbitrary"),
        name="gla_core",
    )(qkvr, g_low, w_gate2, b_gate, norm_w)


def _gla_mixer(h, mix_norm3, layer, j, gla_w_in, gla_w_gate2, gla_b_gate, gla_norm, gla_w_out):
    qkvr, g_low = _gla_in(h, mix_norm3, jnp.swapaxes(gla_w_in, 1, 2), layer, j)
    w_gate2 = jnp.pad(gla_w_gate2[j], ((0, LANES - GLA_GATE_RANK), (0, 0)))
    o = _gla_core(qkvr, g_low, w_gate2, gla_b_gate[j].reshape(1, -1), gla_norm[j].reshape(1, -1))
    return _mm_res(o, gla_w_out, j, h)


def _sgu_in_kernel(h_ref, g_ref, w_ref, b_ref, o_ref, n_ref):
    @pl.when(pl.program_id(1) == 0)
    def _():
        n_ref[...] = _rms(h_ref[...], g_ref[...]).astype(BF16)

    y = _dot(n_ref[...], w_ref[...].astype(BF16)) + b_ref[...]
    o_ref[...] = (0.5 * y * (1.0 + lax.erf(y * (0.5 ** 0.5)))).astype(o_ref.dtype)


def _sgu_in(h, mix_norm3, sgu_w_in, sgu_b_in3, layer, j):
    t = h.shape[0]
    return pl.pallas_call(
        _sgu_in_kernel,
        grid=(t // PROJ_TM, 2 * SGU_WIDTH // PROJ_TN),
        in_specs=[
            pl.BlockSpec((PROJ_TM, D_MODEL), lambda i, c: (i, 0), pipeline_mode=pl.Buffered(1)),
            pl.BlockSpec((None, 1, D_MODEL), lambda i, c: (layer, 0, 0)),
            pl.BlockSpec((None, D_MODEL, PROJ_TN), lambda i, c: (j, 0, c)),
            pl.BlockSpec((None, 1, PROJ_TN), lambda i, c: (j, 0, c)),
        ],
        out_specs=pl.BlockSpec((PROJ_TM, PROJ_TN), lambda i, c: (i, c)),
        out_shape=jax.ShapeDtypeStruct((t, 2 * SGU_WIDTH), BF16),
        scratch_shapes=[pltpu.VMEM((PROJ_TM, D_MODEL), BF16)],
        compiler_params=_params("parallel", "arbitrary"),
        name="sgu_in",
    )(h, mix_norm3, sgu_w_in, sgu_b_in3)


def _sgu_core_kernel(u_ref, v_ref, nw_ref, ws_ref, bs_ref, o_ref):
    q = SGU_CHUNK
    gd = SGU_GROUP_DIM
    row = lax.broadcasted_iota(jnp.int32, (q, q), 0)
    col = lax.broadcasted_iota(jnp.int32, (q, q), 1)
    tril = row >= col
    for ci in range(SGU_CHUNKS_PER_STEP):
        r0 = ci * q
        vn = _rms(v_ref[r0:r0 + q, :].astype(F32), nw_ref[...]).astype(BF16)
        for g in range(SGU_N_GROUPS):
            c0 = g * gd
            wc = jnp.where(tril, ws_ref[g], 0.0).astype(BF16)
            sv = _dot(wc, vn[:, c0:c0 + gd]) + bs_ref[:, g:g + 1]
            o_ref[r0:r0 + q, c0:c0 + gd] = (u_ref[r0:r0 + q, c0:c0 + gd].astype(F32) * sv).astype(o_ref.dtype)


def _sgu_core(zz, norm_w, w_s, b_s_t):
    t = zz.shape[0]
    rows = SGU_CHUNK * SGU_CHUNKS_PER_STEP
    return pl.pallas_call(
        _sgu_core_kernel,
        grid=(t // rows,),
        in_specs=[
            pl.BlockSpec((rows, SGU_WIDTH), lambda c: (c, 0)),
            pl.BlockSpec((rows, SGU_WIDTH), lambda c: (c, 1)),
            pl.BlockSpec((1, SGU_WIDTH), lambda c: (0, 0)),
            pl.BlockSpec((SGU_N_GROUPS, SGU_CHUNK, SGU_CHUNK), lambda c: (0, 0, 0)),
            pl.BlockSpec((SGU_CHUNK, SGU_N_GROUPS), lambda c: (0, 0)),
        ],
        out_specs=pl.BlockSpec((rows, SGU_WIDTH), lambda c: (c, 0)),
        out_shape=jax.ShapeDtypeStruct((t, SGU_WIDTH), BF16),
        compiler_params=_params("parallel"),
        name="sgu_core",
    )(zz, zz, norm_w, w_s, b_s_t)


def _sgu_mixer(h, mix_norm3, layer, j, sgu_w_in, sgu_b_in, sgu_norm, sgu_w_s, sgu_b_s, sgu_w_out):
    zz = _sgu_in(h, mix_norm3, sgu_w_in, sgu_b_in.reshape(sgu_b_in.shape[0], 1, -1), layer, j)
    y = _sgu_core(zz, sgu_norm[j].reshape(1, -1), sgu_w_s[j], sgu_b_s[j].T)
    return _mm_res(y, sgu_w_out, j, h)


def kernel(x, ffn_norm, ffn_w_in, ffn_w_out, mix_norm, ssd_w_in, ssd_conv_w, ssd_conv_b, ssd_dt_bias,
           ssd_a_log, ssd_d, ssd_norm, ssd_w_out, gla_w_in, gla_w_gate2, gla_b_gate, gla_norm, gla_w_out,
           sgu_w_in, sgu_b_in, sgu_norm, sgu_w_s, sgu_b_s, sgu_w_out, final_norm):
    b, t, d = x.shape
    assert (b, t, d) == (1, SEQ, D_MODEL)
    h = x.reshape(t, d)
    ffn_norm4 = ffn_norm.reshape(DEPTH, 2, 1, D_MODEL)
    mix_norm3 = mix_norm.reshape(DEPTH, 1, D_MODEL)
    final_w = final_norm.reshape(1, D_MODEL)
    for i in range(DEPTH):
        h = _ffn(h, ffn_norm4, ffn_w_in, ffn_w_out, final_w, i, 0, False)
        kind, j = i % N_MIXERS, i // N_MIXERS
        if kind == 0:
            h = _ssd_mixer(h, mix_norm3, i, j, ssd_w_in, ssd_conv_w, ssd_conv_b, ssd_dt_bias, ssd_a_log,
                           ssd_d, ssd_norm, ssd_w_out)
        elif kind == 1:
            h = _gla_mixer(h, mix_norm3, i, j, gla_w_in, gla_w_gate2, gla_b_gate, gla_norm, gla_w_out)
        else:
            h = _sgu_mixer(h, mix_norm3, i, j, sgu_w_in, sgu_b_in, sgu_norm, sgu_w_s, sgu_b_s, sgu_w_out)
        h = _ffn(h, ffn_norm4, ffn_w_in, ffn_w_out, final_w, i, 1, i == DEPTH - 1)
    return h.reshape(b, t, d)
```

```python
import functools

import jax
import jax.numpy as jnp
import numpy as np
from jax import lax
from jax.experimental import pallas as pl
from jax.experimental.pallas import tpu as pltpu

F32 = jnp.float32
BF16 = jnp.bfloat16

D_MODEL = 2048
SEQ = 8192
DEPTH = 4
N_MIXERS = 3
NORM_EPS = 1e-6
D_FF = 5632

SSD_D_INNER = 4096
SSD_HEAD_DIM = 64
SSD_N_HEADS = 64
SSD_N_GROUPS = 8
SSD_HEADS_PER_GROUP = 8
SSD_D_STATE = 128
SSD_CONV = 4
SSD_CHUNK = 128
SSD_CONV_DIM = 6144
SSD_GROUP_WIDTH = SSD_HEADS_PER_GROUP * SSD_HEAD_DIM
SSD_MAIN_DIM = SSD_D_INNER + SSD_CONV_DIM

GLA_N_HEADS = 4
GLA_D_K = 1024
GLA_D_V = 2048
GLA_HEAD_K = 256
GLA_HEAD_V = 512
GLA_GATE_RANK = 16
GLA_GATE_TAU = 16.0
GLA_CHUNK = 64
GLA_MAIN_DIM = 2 * GLA_D_K + 2 * GLA_D_V

SGU_WIDTH = 4096
SGU_N_GROUPS = 8
SGU_GROUP_DIM = 512
SGU_CHUNK = 128

FFN_TM = 1024
FFN_TF = 512
PROJ_TM = 1024
PROJ_TN = 1024
OUT_TM = 512
OUT_TN = 1024
SSD_CHUNKS_PER_STEP = 2
GLA_CHUNKS_PER_STEP = 4
SGU_CHUNKS_PER_STEP = 4
LANES = 128
BF16_ROWS = 16
CONV_TAIL = BF16_ROWS
CONV_K = -(-(SSD_CONV - 1) * (SSD_CHUNK + CONV_TAIL) // LANES) * LANES
LOG2E = 1.4426950408889634

VMEM_LIMIT = 56 * 1024 * 1024
FFN_VMEM_LIMIT = 62 * 1024 * 1024


def _silu(x):
    hx = 0.5 * x
    return hx * jnp.tanh(hx) + hx


def _softplus(x):
    return jnp.maximum(x, 0.0) + jnp.log1p(jnp.exp(-jnp.abs(x)))


def _rms(x, w):
    return x * lax.rsqrt(jnp.mean(x * x, axis=-1, keepdims=True) + NORM_EPS) * w


def _dot(a, b):
    return jnp.dot(a, b, preferred_element_type=F32)


def _dot_nt(a, b):
    return lax.dot_general(a, b, (((1,), (1,)), ((), ())), preferred_element_type=F32)


def _dot_tn(a, b):
    return lax.dot_general(a, b, (((0,), (0,)), ((), ())), preferred_element_type=F32)


def _split3(x):
    x1 = x.astype(BF16)
    r1 = x - x1.astype(F32)
    x2 = r1.astype(BF16)
    r2 = r1 - x2.astype(F32)
    return x1, x2, r2.astype(BF16)


def _params(*sem):
    return pltpu.CompilerParams(dimension_semantics=sem, vmem_limit_bytes=VMEM_LIMIT)


def _ffn_kernel(h_hbm, g_ref, wg_ref, wu_ref, wo_ref, fw_ref, o_hbm, n_ref, acc_ref, in_sem, out_sem, *, final):
    i = pl.program_id(0)
    j = pl.program_id(1)
    n_tiles = pl.num_programs(0)
    slot = i % 2
    other = 1 - slot

    def h_copy(tile, s):
        return pltpu.make_async_copy(h_hbm.at[pl.ds(tile * FFN_TM, FFN_TM), :], acc_ref.at[s], in_sem.at[s])

    def o_copy(tile, s):
        return pltpu.make_async_copy(acc_ref.at[s], o_hbm.at[pl.ds(tile * FFN_TM, FFN_TM), :], out_sem.at[s])

    @pl.when(j == 0)
    def _():
        @pl.when(i == 0)
        def _():
            h_copy(0, 0).start()

        h_copy(i, slot).wait()
        n_ref[...] = _rms(acc_ref[slot], g_ref[...]).astype(BF16)

    @pl.when(jnp.logical_and(j == 1, i + 1 < n_tiles))
    def _():
        @pl.when(i >= 1)
        def _():
            o_copy(i - 1, other).wait()

        h_copy(i + 1, other).start()

    n = n_ref[...]
    gate = _dot(n, wg_ref[...].astype(BF16))
    up = _dot(n, wu_ref[...].astype(BF16))
    act = (0.5 * _silu(gate) * up).astype(BF16)
    acc_ref[slot] += _dot(act, wo_ref[...].astype(BF16))

    @pl.when(j == pl.num_programs(1) - 1)
    def _():
        if final:
            acc_ref[slot] = _rms(acc_ref[slot], fw_ref[...])
        o_copy(i, slot).start()

        @pl.when(i == n_tiles - 1)
        def _():
            o_copy(i, slot).wait()

            @pl.when(i >= 1)
            def _():
                o_copy(i - 1, other).wait()


def _ffn(h, ffn_norm4, ffn_w_in, ffn_w_out, final_w, layer, half, final):
    t = h.shape[0]
    nf = D_FF // FFN_TF
    return pl.pallas_call(
        functools.partial(_ffn_kernel, final=final),
        grid=(t // FFN_TM, nf),
        in_specs=[
            pl.BlockSpec(memory_space=pl.ANY),
            pl.BlockSpec((None, None, 1, D_MODEL), lambda i, j: (layer, half, 0, 0)),
            pl.BlockSpec((None, None, D_MODEL, FFN_TF), lambda i, j: (layer, half, 0, j)),
            pl.BlockSpec((None, None, D_MODEL, FFN_TF), lambda i, j: (layer, half, 0, j + nf)),
            pl.BlockSpec((None, None, FFN_TF, D_MODEL), lambda i, j: (layer, half, j, 0)),
            pl.BlockSpec((1, D_MODEL), lambda i, j: (0, 0)),
        ],
        out_specs=pl.BlockSpec(memory_space=pl.ANY),
        out_shape=jax.ShapeDtypeStruct((t, D_MODEL), F32),
        scratch_shapes=[
            pltpu.VMEM((FFN_TM, D_MODEL), BF16),
            pltpu.VMEM((2, FFN_TM, D_MODEL), F32),
            pltpu.SemaphoreType.DMA((2,)),
            pltpu.SemaphoreType.DMA((2,)),
        ],
        compiler_params=pltpu.CompilerParams(dimension_semantics=("arbitrary", "arbitrary"),
                                             vmem_limit_bytes=FFN_VMEM_LIMIT),
        name="ffn",
    )(h, ffn_norm4, ffn_w_in, ffn_w_in, ffn_w_out, final_w)


def _mm_res_kernel(a_ref, w_ref, r_ref, o_ref, wb_ref):
    @pl.when(pl.program_id(1) == 0)
    def _():
        wb_ref[...] = w_ref[...].astype(BF16)

    o_ref[...] = r_ref[...] + _dot(a_ref[...], wb_ref[...])


def _mm_res(a, w, layer, res):
    t, k = a.shape
    return pl.pallas_call(
        _mm_res_kernel,
        grid=(D_MODEL // OUT_TN, t // OUT_TM),
        in_specs=[
            pl.BlockSpec((OUT_TM, k), lambda j, i: (i, 0)),
            pl.BlockSpec((None, k, OUT_TN), lambda j, i: (layer, 0, j), pipeline_mode=pl.Buffered(1)),
            pl.BlockSpec((OUT_TM, OUT_TN), lambda j, i: (i, j)),
        ],
        out_specs=pl.BlockSpec((OUT_TM, OUT_TN), lambda j, i: (i, j)),
        out_shape=jax.ShapeDtypeStruct((t, D_MODEL), F32),
        scratch_shapes=[pltpu.VMEM((k, OUT_TN), BF16)],
        compiler_params=_params("parallel", "arbitrary"),
        name="out_proj",
    )(a, w, res)


def _narrow_rows(w_ref, valid):
    row = lax.broadcasted_iota(jnp.int32, (LANES, 1), 0)
    return jnp.where(row < valid, w_ref[...], 0.0).astype(BF16)


def _ssd_in_kernel(h_ref, g_ref, w_ref, wdt_ref, o_ref, dt_ref, n_ref):
    @pl.when(pl.program_id(1) == 0)
    def _():
        n = _rms(h_ref[...], g_ref[...]).astype(BF16)
        n_ref[...] = n
        dt_ref[...] = _dot_nt(n, _narrow_rows(wdt_ref, SSD_N_HEADS))

    o_ref[...] = _dot_nt(n_ref[...], w_ref[...].astype(BF16)).astype(o_ref.dtype)


def _ssd_in(h, mix_norm3, ssd_w_in_t, layer, j):
    t = h.shape[0]
    return pl.pallas_call(
        _ssd_in_kernel,
        grid=(t // PROJ_TM, SSD_MAIN_DIM // PROJ_TN),
        in_specs=[
            pl.BlockSpec((PROJ_TM, D_MODEL), lambda i, c: (i, 0)),
            pl.BlockSpec((None, 1, D_MODEL), lambda i, c: (layer, 0, 0)),
            pl.BlockSpec((None, PROJ_TN, D_MODEL), lambda i, c: (j, c, 0)),
            pl.BlockSpec((None, LANES, D_MODEL), lambda i, c: (j, SSD_MAIN_DIM // LANES, 0)),
        ],
        out_specs=[
            pl.BlockSpec((PROJ_TM, PROJ_TN), lambda i, c: (i, c)),
            pl.BlockSpec((PROJ_TM, LANES), lambda i, c: (i, 0)),
        ],
        out_shape=[
            jax.ShapeDtypeStruct((t, SSD_MAIN_DIM), BF16),
            jax.ShapeDtypeStruct((t, LANES), F32),
        ],
        scratch_shapes=[pltpu.VMEM((PROJ_TM, D_MODEL), BF16)],
        compiler_params=_params("parallel", "arbitrary"),
        name="ssd_in",
    )(h, mix_norm3, ssd_w_in_t, ssd_w_in_t)


def _conv_shift_matrix():
    q, blk = SSD_CHUNK, SSD_CHUNK + CONV_TAIL
    s = np.zeros((q, CONV_K), np.float32)
    for k in range(SSD_CONV - 1):
        shift = SSD_CONV - 1 - k
        for t in range(q):
            src = t - shift
            s[t, k * blk + (src if src >= 0 else q + CONV_TAIL + src)] = 1.0
    return jnp.asarray(s, BF16)


def _ssd_core_kernel(z_ref, xs_ref, bc_ref, dt_ref, cw_ref, cb_ref, dtb_ref, al_ref, dex_ref, nw_ref,
                     e_ref, e3_ref, sh_ref, o_ref, st_ref, tail_ref):
    q = SSD_CHUNK
    gw = SSD_GROUP_WIDTH
    ns = SSD_D_STATE

    @pl.when(pl.program_id(0) == 0)
    def _():
        st_ref[...] = jnp.zeros_like(st_ref)
        tail_ref[...] = jnp.zeros_like(tail_ref)

    row = lax.broadcasted_iota(jnp.int32, (q, q), 0)
    col = lax.broadcasted_iota(jnp.int32, (q, q), 1)
    tril = row >= col
    lower = jnp.where(tril, 1.0, 0.0).astype(BF16)
    lane = lax.broadcasted_iota(jnp.int32, (q, 2 * SSD_HEAD_DIM), 1)
    even_head = lane < SSD_HEAD_DIM

    for ci in range(SSD_CHUNKS_PER_STEP):
        r0 = ci * q

        def conv(src_ref, src_lo, lo, width):
            cur = src_ref[r0:r0 + q, src_lo:src_lo + width]
            if ci == 0:
                tail = tail_ref[:, lo:lo + width]
            else:
                tail = src_ref[r0 - CONV_TAIL:r0, src_lo:src_lo + width]
            parts = []
            for k in range(SSD_CONV - 1):
                wk = cw_ref[k:k + 1, lo:lo + width].astype(BF16)
                parts += [cur * wk, tail * wk]
            parts.append(jnp.zeros((CONV_K - (SSD_CONV - 1) * (q + CONV_TAIL), width), BF16))
            acc = _dot(sh_ref[...], jnp.concatenate(parts, axis=0))
            acc = acc + cur.astype(F32) * cw_ref[SSD_CONV - 1:SSD_CONV, lo:lo + width] + cb_ref[:, lo:lo + width]
            return _silu(acc)

        dt = _softplus(dt_ref[r0:r0 + q, :] + dtb_ref[...])
        da = dt * (-LOG2E * jnp.exp(al_ref[...]))
        acum = sum(_dot(lower, p) for p in _split3(da))
        acum_t = acum.T
        acum3 = jnp.concatenate(_split3(acum), axis=1)
        dt16 = dt.astype(BF16)

        for g in range(SSD_N_GROUPS):
            c0 = g * gw
            ae = _dot(acum3, e3_ref[:, c0:c0 + gw])
            dte = _dot(dt16, e_ref[:, c0:c0 + gw])
            xs = conv(xs_ref, c0, c0, gw)
            b0 = g * ns
            c1 = SSD_N_GROUPS * ns + g * ns
            bg = conv(bc_ref, b0, SSD_D_INNER + b0, ns).astype(BF16)
            cg = conv(bc_ref, c1, SSD_D_INNER + c1, ns).astype(BF16)
            cbm = _dot_nt(cg, bg)
            xdt = xs * dte

            st = st_ref[g]
            y = _dot(cg, st.astype(BF16)) * jnp.exp2(ae)

            pieces = []
            for pr in range(SSD_HEADS_PER_GROUP // 2):
                def scores(hh):
                    seg = acum[:, hh:hh + 1] - acum_t[hh:hh + 1, :]
                    return (cbm * jnp.exp2(jnp.where(tril, seg, -jnp.inf))).astype(BF16)

                h0 = g * SSD_HEADS_PER_GROUP + 2 * pr
                lhs = jnp.concatenate([scores(h0), scores(h0 + 1)], axis=1)
                l0 = pr * 2 * SSD_HEAD_DIM
                xp = xdt[:, l0:l0 + 2 * SSD_HEAD_DIM]
                rhs = jnp.concatenate([jnp.where(even_head, xp, 0.0),
                                       jnp.where(even_head, 0.0, xp)], axis=0)
                pieces.append(_dot(lhs, rhs.astype(BF16)))
            y = y + jnp.concatenate(pieces, axis=1)

            last = ae[q - 1:q, :]
            to_end = jnp.exp2(last - ae)
            st_ref[g] = st * jnp.exp2(last) + _dot_tn(bg, (xdt * to_end).astype(BF16))

            y = y + dex_ref[:, c0:c0 + gw] * xs
            gated = y * _silu(z_ref[r0:r0 + q, c0:c0 + gw].astype(F32))
            o_ref[r0:r0 + q, c0:c0 + gw] = _rms(gated, nw_ref[:, c0:c0 + gw]).astype(o_ref.dtype)

    rows = SSD_CHUNKS_PER_STEP * q
    tail_ref[:, 0:SSD_D_INNER] = xs_ref[rows - CONV_TAIL:rows, :]
    tail_ref[:, SSD_D_INNER:SSD_CONV_DIM] = bc_ref[rows - CONV_TAIL:rows, :]


def _ssd_core(zx, dt, conv_w_t, conv_b, dt_bias, a_log, d_exp, norm_w, expand):
    t = zx.shape[0]
    q = SSD_CHUNK
    rows = SSD_CHUNKS_PER_STEP * q
    full = lambda shape: pl.BlockSpec(shape, lambda c: (0,) * len(shape))
    pad_heads = lambda v: jnp.pad(v, (0, LANES - SSD_N_HEADS)).reshape(1, LANES)
    return pl.pallas_call(
        _ssd_core_kernel,
        grid=(t // rows,),
        in_specs=[
            pl.BlockSpec((rows, SSD_D_INNER), lambda c: (c, 0)),
            pl.BlockSpec((rows, SSD_D_INNER), lambda c: (c, 1)),
            pl.BlockSpec((rows, 2 * SSD_N_GROUPS * SSD_D_STATE), lambda c: (c, 4)),
            pl.BlockSpec((rows, LANES), lambda c: (c, 0)),
            full((SSD_CONV, SSD_CONV_DIM)),
            full((1, SSD_CONV_DIM)),
            full((1, LANES)),
            full((1, LANES)),
            full((1, SSD_D_INNER)),
            full((1, SSD_D_INNER)),
            full((LANES, SSD_D_INNER)),
            full((3 * LANES, SSD_D_INNER)),
            full((q, CONV_K)),
        ],
        out_specs=pl.BlockSpec((rows, SSD_D_INNER), lambda c: (c, 0)),
        out_shape=jax.ShapeDtypeStruct((t, SSD_D_INNER), BF16),
        scratch_shapes=[
            pltpu.VMEM((SSD_N_GROUPS, SSD_D_STATE, SSD_GROUP_WIDTH), F32),
            pltpu.VMEM((CONV_TAIL, SSD_CONV_DIM), BF16),
        ],
        compiler_params=_params("arbitrary"),
        name="ssd_core",
    )(zx, zx, zx, dt, conv_w_t, conv_b, pad_heads(dt_bias), pad_heads(a_log), d_exp, norm_w,
      expand, jnp.tile(expand, (3, 1)), _conv_shift_matrix())


def _ssd_mixer(h, mix_norm3, layer, j, ssd_w_in, ssd_conv_w, ssd_conv_b, ssd_dt_bias, ssd_a_log,
               ssd_d, ssd_norm, ssd_w_out):
    zx, dt = _ssd_in(h, mix_norm3, jnp.swapaxes(ssd_w_in, 1, 2), layer, j)
    head_of_channel = jnp.arange(SSD_D_INNER) // SSD_HEAD_DIM
    expand = (jnp.arange(LANES)[:, None] == head_of_channel[None, :]).astype(BF16)
    d_exp = jnp.repeat(ssd_d[j], SSD_HEAD_DIM).reshape(1, SSD_D_INNER)
    yn = _ssd_core(zx, dt, ssd_conv_w[j].T, ssd_conv_b[j].reshape(1, -1), ssd_dt_bias[j],
                   ssd_a_log[j], d_exp, ssd_norm[j].reshape(1, -1), expand)
    return _mm_res(yn, ssd_w_out, j, h)


def _gla_in_kernel(h_ref, g_ref, w_ref, wl_ref, o_ref, gl_ref, n_ref):
    @pl.when(pl.program_id(1) == 0)
    def _():
        n = _rms(h_ref[...], g_ref[...]).astype(BF16)
        n_ref[...] = n
        gl_ref[...] = _dot_nt(n, _narrow_rows(wl_ref, GLA_GATE_RANK))

    o_ref[...] = _dot_nt(n_ref[...], w_ref[...].astype(BF16)).astype(o_ref.dtype)


def _gla_in(h, mix_norm3, gla_w_in_t, layer, j):
    t = h.shape[0]
    return pl.pallas_call(
        _gla_in_kernel,
        grid=(t // PROJ_TM, GLA_MAIN_DIM // PROJ_TN),
        in_specs=[
            pl.BlockSpec((PROJ_TM, D_MODEL), lambda i, c: (i, 0)),
            pl.BlockSpec((None, 1, D_MODEL), lambda i, c: (layer, 0, 0)),
            pl.BlockSpec((None, PROJ_TN, D_MODEL), lambda i, c: (j, c, 0)),
            pl.BlockSpec((None, LANES, D_MODEL), lambda i, c: (j, GLA_MAIN_DIM // LANES, 0)),
        ],
        out_specs=[
            pl.BlockSpec((PROJ_TM, PROJ_TN), lambda i, c: (i, c)),
            pl.BlockSpec((PROJ_TM, LANES), lambda i, c: (i, 0)),
        ],
        out_shape=[
            jax.ShapeDtypeStruct((t, GLA_MAIN_DIM), BF16),
            jax.ShapeDtypeStruct((t, LANES), F32),
        ],
        scratch_shapes=[pltpu.VMEM((PROJ_TM, D_MODEL), BF16)],
        compiler_params=_params("parallel", "arbitrary"),
        name="gla_in",
    )(h, mix_norm3, gla_w_in_t, gla_w_in_t)


def _gla_core_kernel(x_ref, gl_ref, w2_ref, bg_ref, nw_ref, o_ref, st_ref):
    q = GLA_CHUNK
    hk, hv = GLA_HEAD_K, GLA_HEAD_V

    @pl.when(pl.program_id(0) == 0)
    def _():
        st_ref[...] = jnp.zeros_like(st_ref)

    row = lax.broadcasted_iota(jnp.int32, (q, q), 0)
    col = lax.broadcasted_iota(jnp.int32, (q, q), 1)
    tril = row >= col
    lower = jnp.where(tril, 1.0, 0.0).astype(BF16)
    w2 = w2_ref[...].astype(BF16)

    for ci in range(GLA_CHUNKS_PER_STEP):
        r0 = ci * q
        lg = _dot(gl_ref[r0:r0 + q, :].astype(BF16), w2) + bg_ref[...]
        log_a = (jnp.minimum(lg, 0.0) - jnp.log1p(jnp.exp(-jnp.abs(lg)))) * (LOG2E / GLA_GATE_TAU)
        bcum = sum(_dot(lower, p) for p in _split3(log_a))
        for hd in range(GLA_N_HEADS):
            k0 = hd * hk
            v0 = 2 * GLA_D_K + hd * hv
            qh = x_ref[r0:r0 + q, k0:k0 + hk].astype(F32) * (hk ** -0.5)
            kh = x_ref[r0:r0 + q, GLA_D_K + k0:GLA_D_K + k0 + hk].astype(F32)
            vh = x_ref[r0:r0 + q, v0:v0 + hv]
            rh = x_ref[r0:r0 + q, v0 + GLA_D_V:v0 + GLA_D_V + hv].astype(F32)
            bc = bcum[:, k0:k0 + hk]
            mid = bc[q // 2:q // 2 + 1, :]
            end = bc[q - 1:q, :]
            att = _dot_nt((qh * jnp.exp2(bc - mid)).astype(BF16), (kh * jnp.exp2(mid - bc)).astype(BF16))
            att = jnp.where(tril, att, 0.0)
            o = _dot(att.astype(BF16), vh)
            st = st_ref[hd]
            o = o + _dot_nt((qh * jnp.exp2(bc)).astype(BF16), st.astype(BF16))
            k_out = (kh * jnp.exp2(end - bc)).astype(BF16)
            st_ref[hd] = st * jnp.exp2(end) + _dot_tn(vh, k_out)
            o = _rms(o, nw_ref[...]) * _silu(rh)
            o_ref[r0:r0 + q, hd * hv:(hd + 1) * hv] = o.astype(o_ref.dtype)


def _gla_core(qkvr, g_low, w_gate2, b_gate, norm_w):
    t = qkvr.shape[0]
    rows = GLA_CHUNK * GLA_CHUNKS_PER_STEP
    return pl.pallas_call(
        _gla_core_kernel,
        grid=(t // rows,),
        in_specs=[
            pl.BlockSpec((rows, GLA_MAIN_DIM), lambda c: (c, 0)),
            pl.BlockSpec((rows, LANES), lambda c: (c, 0)),
            pl.BlockSpec((LANES, GLA_D_K), lambda c: (0, 0)),
            pl.BlockSpec((1, GLA_D_K), lambda c: (0, 0)),
            pl.BlockSpec((1, GLA_HEAD_V), lambda c: (0, 0)),
        ],
        out_specs=pl.BlockSpec((rows, GLA_D_V), lambda c: (c, 0)),
        out_shape=jax.ShapeDtypeStruct((t, GLA_D_V), BF16),
        scratch_shapes=[pltpu.VMEM((GLA_N_HEADS, GLA_HEAD_V, GLA_HEAD_K), F32)],
        compiler_params=_params("arbitrary"),
        name="gla_core",
    )(qkvr, g_low, w_gate2, b_gate, norm_w)


def _gla_mixer(h, mix_norm3, layer, j, gla_w_in, gla_w_gate2, gla_b_gate, gla_norm, gla_w_out):
    qkvr, g_low = _gla_in(h, mix_norm3, jnp.swapaxes(gla_w_in, 1, 2), layer, j)
    w_gate2 = jnp.pad(gla_w_gate2[j], ((0, LANES - GLA_GATE_RANK), (0, 0)))
    o = _gla_core(qkvr, g_low, w_gate2, gla_b_gate[j].reshape(1, -1), gla_norm[j].reshape(1, -1))
    return _mm_res(o, gla_w_out, j, h)


def _sgu_in_kernel(h_ref, g_ref, w_ref, b_ref, o_ref, n_ref):
    @pl.when(pl.program_id(1) == 0)
    def _():
        n_ref[...] = _rms(h_ref[...], g_ref[...]).astype(BF16)

    y = _dot(n_ref[...], w_ref[...].astype(BF16)) + b_ref[...]
    o_ref[...] = (0.5 * y * (1.0 + lax.erf(y * (0.5 ** 0.5)))).astype(o_ref.dtype)


def _sgu_in(h, mix_norm3, sgu_w_in, sgu_b_in3, layer, j):
    t = h.shape[0]
    return pl.pallas_call(
        _sgu_in_kernel,
        grid=(t // PROJ_TM, 2 * SGU_WIDTH // PROJ_TN),
        in_specs=[
            pl.BlockSpec((PROJ_TM, D_MODEL), lambda i, c: (i, 0)),
            pl.BlockSpec((None, 1, D_MODEL), lambda i, c: (layer, 0, 0)),
            pl.BlockSpec((None, D_MODEL, PROJ_TN), lambda i, c: (j, 0, c)),
            pl.BlockSpec((None, 1, PROJ_TN), lambda i, c: (j, 0, c)),
        ],
        out_specs=pl.BlockSpec((PROJ_TM, PROJ_TN), lambda i, c: (i, c)),
        out_shape=jax.ShapeDtypeStruct((t, 2 * SGU_WIDTH), BF16),
        scratch_shapes=[pltpu.VMEM((PROJ_TM, D_MODEL), BF16)],
        compiler_params=_params("parallel", "arbitrary"),
        name="sgu_in",
    )(h, mix_norm3, sgu_w_in, sgu_b_in3)


def _sgu_core_kernel(u_ref, v_ref, nw_ref, ws_ref, bs_ref, o_ref):
    q = SGU_CHUNK
    gd = SGU_GROUP_DIM
    row = lax.broadcasted_iota(jnp.int32, (q, q), 0)
    col = lax.broadcasted_iota(jnp.int32, (q, q), 1)
    tril = row >= col
    for ci in range(SGU_CHUNKS_PER_STEP):
        r0 = ci * q
        vn = _rms(v_ref[r0:r0 + q, :].astype(F32), nw_ref[...]).astype(BF16)
        for g in range(SGU_N_GROUPS):
            c0 = g * gd
            wc = jnp.where(tril, ws_ref[g], 0.0).astype(BF16)
            sv = _dot(wc, vn[:, c0:c0 + gd]) + bs_ref[:, g:g + 1]
            o_ref[r0:r0 + q, c0:c0 + gd] = (u_ref[r0:r0 + q, c0:c0 + gd].astype(F32) * sv).astype(o_ref.dtype)


def _sgu_core(zz, norm_w, w_s, b_s_t):
    t = zz.shape[0]
    rows = SGU_CHUNK * SGU_CHUNKS_PER_STEP
    return pl.pallas_call(
        _sgu_core_kernel,
        grid=(t // rows,),
        in_specs=[
            pl.BlockSpec((rows, SGU_WIDTH), lambda c: (c, 0)),
            pl.BlockSpec((rows, SGU_WIDTH), lambda c: (c, 1)),
            pl.BlockSpec((1, SGU_WIDTH), lambda c: (0, 0)),
            pl.BlockSpec((SGU_N_GROUPS, SGU_CHUNK, SGU_CHUNK), lambda c: (0, 0, 0)),
            pl.BlockSpec((SGU_CHUNK, SGU_N_GROUPS), lambda c: (0, 0)),
        ],
        out_specs=pl.BlockSpec((rows, SGU_WIDTH), lambda c: (c, 0)),
        out_shape=jax.ShapeDtypeStruct((t, SGU_WIDTH), BF16),
        compiler_params=_params("parallel"),
        name="sgu_core",
    )(zz, zz, norm_w, w_s, b_s_t)


def _sgu_mixer(h, mix_norm3, layer, j, sgu_w_in, sgu_b_in, sgu_norm, sgu_w_s, sgu_b_s, sgu_w_out):
    zz = _sgu_in(h, mix_norm3, sgu_w_in, sgu_b_in.reshape(sgu_b_in.shape[0], 1, -1), layer, j)
    y = _sgu_core(zz, sgu_norm[j].reshape(1, -1), sgu_w_s[j], sgu_b_s[j].T)
    return _mm_res(y, sgu_w_out, j, h)


def kernel(x, ffn_norm, ffn_w_in, ffn_w_out, mix_norm, ssd_w_in, ssd_conv_w, ssd_conv_b, ssd_dt_bias,
           ssd_a_log, ssd_d, ssd_norm, ssd_w_out, gla_w_in, gla_w_gate2, gla_b_gate, gla_norm, gla_w_out,
           sgu_w_in, sgu_b_in, sgu_norm, sgu_w_s, sgu_b_s, sgu_w_out, final_norm):
    b, t, d = x.shape
    assert (b, t, d) == (1, SEQ, D_MODEL)
    h = x.reshape(t, d)
    ffn_norm4 = ffn_norm.reshape(DEPTH, 2, 1, D_MODEL)
    mix_norm3 = mix_norm.reshape(DEPTH, 1, D_MODEL)
    final_w = final_norm.reshape(1, D_MODEL)
    for i in range(DEPTH):
        h = _ffn(h, ffn_norm4, ffn_w_in, ffn_w_out, final_w, i, 0, False)
        kind, j = i % N_MIXERS, i // N_MIXERS
        if kind == 0:
            h = _ssd_mixer(h, mix_norm3, i, j, ssd_w_in, ssd_conv_w, ssd_conv_b, ssd_dt_bias, ssd_a_log,
                           ssd_d, ssd_norm, ssd_w_out)
        elif kind == 1:
            h = _gla_mixer(h, mix_norm3, i, j, gla_w_in, gla_w_gate2, gla_b_gate, gla_norm, gla_w_out)
        else:
            h = _sgu_mixer(h, mix_norm3, i, j, sgu_w_in, sgu_b_in, sgu_norm, sgu_w_s, sgu_b_s, sgu_w_out)
        h = _ffn(h, ffn_norm4, ffn_w_in, ffn_w_out, final_w, i, 1, i == DEPTH - 1)
    return h.reshape(b, t, d)
```

```python
import functools

import jax
import jax.numpy as jnp
import numpy as np
from jax import lax
from jax.experimental import pallas as pl
from jax.experimental.pallas import tpu as pltpu

F32 = jnp.float32
BF16 = jnp.bfloat16

D_MODEL = 2048
SEQ = 8192
DEPTH = 4
N_MIXERS = 3
NORM_EPS = 1e-6
D_FF = 5632

SSD_D_INNER = 4096
SSD_HEAD_DIM = 64
SSD_N_HEADS = 64
SSD_N_GROUPS = 8
SSD_HEADS_PER_GROUP = 8
SSD_D_STATE = 128
SSD_CONV = 4
SSD_CHUNK = 128
SSD_CONV_DIM = 6144
SSD_GROUP_WIDTH = SSD_HEADS_PER_GROUP * SSD_HEAD_DIM
SSD_MAIN_DIM = SSD_D_INNER + SSD_CONV_DIM

GLA_N_HEADS = 4
GLA_D_K = 1024
GLA_D_V = 2048
GLA_HEAD_K = 256
GLA_HEAD_V = 512
GLA_GATE_RANK = 16
GLA_GATE_TAU = 16.0
GLA_CHUNK = 64
GLA_MAIN_DIM = 2 * GLA_D_K + 2 * GLA_D_V

SGU_WIDTH = 4096
SGU_N_GROUPS = 8
SGU_GROUP_DIM = 512
SGU_CHUNK = 128

FFN_TM = 1024
FFN_TF = 512
PROJ_TM = 2048
PROJ_TN = 512
OUT_TM = 512
OUT_TN = 1024
SSD_CHUNKS_PER_STEP = 2
GLA_CHUNKS_PER_STEP = 4
SGU_CHUNKS_PER_STEP = 4
LANES = 128
BF16_ROWS = 16
CONV_TAIL = BF16_ROWS
CONV_K = -(-(SSD_CONV - 1) * (SSD_CHUNK + CONV_TAIL) // LANES) * LANES
LOG2E = 1.4426950408889634

VMEM_LIMIT = 56 * 1024 * 1024
BIG_VMEM_LIMIT = 62 * 1024 * 1024


def _silu(x):
    hx = 0.5 * x
    return hx * jnp.tanh(hx) + hx


def _softplus(x):
    return jnp.maximum(x, 0.0) + jnp.log1p(jnp.exp(-jnp.abs(x)))


def _rms(x, w):
    return x * lax.rsqrt(jnp.mean(x * x, axis=-1, keepdims=True) + NORM_EPS) * w


def _dot(a, b):
    return jnp.dot(a, b, preferred_element_type=F32)


def _dot_nt(a, b):
    return lax.dot_general(a, b, (((1,), (1,)), ((), ())), preferred_element_type=F32)


def _dot_tn(a, b):
    return lax.dot_general(a, b, (((0,), (0,)), ((), ())), preferred_element_type=F32)


def _split3(x):
    x1 = x.astype(BF16)
    r1 = x - x1.astype(F32)
    x2 = r1.astype(BF16)
    r2 = r1 - x2.astype(F32)
    return x1, x2, r2.astype(BF16)


def _params(*sem):
    return pltpu.CompilerParams(dimension_semantics=sem, vmem_limit_bytes=VMEM_LIMIT)


def _ffn_kernel(h_hbm, g_ref, wg_ref, wu_ref, wo_ref, fw_ref, o_hbm, n_ref, acc_ref, in_sem, out_sem, *, final):
    i = pl.program_id(0)
    j = pl.program_id(1)
    n_tiles = pl.num_programs(0)
    slot = i % 2
    other = 1 - slot

    def h_copy(tile, s):
        return pltpu.make_async_copy(h_hbm.at[pl.ds(tile * FFN_TM, FFN_TM), :], acc_ref.at[s], in_sem.at[s])

    def o_copy(tile, s):
        return pltpu.make_async_copy(acc_ref.at[s], o_hbm.at[pl.ds(tile * FFN_TM, FFN_TM), :], out_sem.at[s])

    @pl.when(j == 0)
    def _():
        @pl.when(i == 0)
        def _():
            h_copy(0, 0).start()

        h_copy(i, slot).wait()
        n_ref[...] = _rms(acc_ref[slot], g_ref[...]).astype(BF16)

    @pl.when(jnp.logical_and(j == 1, i + 1 < n_tiles))
    def _():
        @pl.when(i >= 1)
        def _():
            o_copy(i - 1, other).wait()

        h_copy(i + 1, other).start()

    n = n_ref[...]
    gate = _dot(n, wg_ref[...].astype(BF16))
    up = _dot(n, wu_ref[...].astype(BF16))
    act = (0.5 * _silu(gate) * up).astype(BF16)
    acc_ref[slot] += _dot(act, wo_ref[...].astype(BF16))

    @pl.when(j == pl.num_programs(1) - 1)
    def _():
        if final:
            acc_ref[slot] = _rms(acc_ref[slot], fw_ref[...])
        o_copy(i, slot).start()

        @pl.when(i == n_tiles - 1)
        def _():
            o_copy(i, slot).wait()

            @pl.when(i >= 1)
            def _():
                o_copy(i - 1, other).wait()


def _ffn(h, ffn_norm4, ffn_w_in, ffn_w_out, final_w, layer, half, final):
    t = h.shape[0]
    nf = D_FF // FFN_TF
    return pl.pallas_call(
        functools.partial(_ffn_kernel, final=final),
        grid=(t // FFN_TM, nf),
        in_specs=[
            pl.BlockSpec(memory_space=pl.ANY),
            pl.BlockSpec((None, None, 1, D_MODEL), lambda i, j: (layer, half, 0, 0)),
            pl.BlockSpec((None, None, D_MODEL, FFN_TF), lambda i, j: (layer, half, 0, j)),
            pl.BlockSpec((None, None, D_MODEL, FFN_TF), lambda i, j: (layer, half, 0, j + nf)),
            pl.BlockSpec((None, None, FFN_TF, D_MODEL), lambda i, j: (layer, half, j, 0)),
            pl.BlockSpec((1, D_MODEL), lambda i, j: (0, 0)),
        ],
        out_specs=pl.BlockSpec(memory_space=pl.ANY),
        out_shape=jax.ShapeDtypeStruct((t, D_MODEL), F32),
        scratch_shapes=[
            pltpu.VMEM((FFN_TM, D_MODEL), BF16),
            pltpu.VMEM((2, FFN_TM, D_MODEL), F32),
            pltpu.SemaphoreType.DMA((2,)),
            pltpu.SemaphoreType.DMA((2,)),
        ],
        compiler_params=pltpu.CompilerParams(dimension_semantics=("arbitrary", "arbitrary"),
                                             vmem_limit_bytes=BIG_VMEM_LIMIT),
        name="ffn",
    )(h, ffn_norm4, ffn_w_in, ffn_w_in, ffn_w_out, final_w)


def _mm_res_kernel(a_ref, w_ref, r_ref, o_ref, wb_ref):
    @pl.when(pl.program_id(1) == 0)
    def _():
        wb_ref[...] = w_ref[...].astype(BF16)

    o_ref[...] = r_ref[...] + _dot(a_ref[...], wb_ref[...])


def _mm_res(a, w, layer, res):
    t, k = a.shape
    return pl.pallas_call(
        _mm_res_kernel,
        grid=(D_MODEL // OUT_TN, t // OUT_TM),
        in_specs=[
            pl.BlockSpec((OUT_TM, k), lambda j, i: (i, 0)),
            pl.BlockSpec((None, k, OUT_TN), lambda j, i: (layer, 0, j)),
            pl.BlockSpec((OUT_TM, OUT_TN), lambda j, i: (i, j)),
        ],
        out_specs=pl.BlockSpec((OUT_TM, OUT_TN), lambda j, i: (i, j)),
        out_shape=jax.ShapeDtypeStruct((t, D_MODEL), F32),
        scratch_shapes=[pltpu.VMEM((k, OUT_TN), BF16)],
        compiler_params=pltpu.CompilerParams(dimension_semantics=("parallel", "arbitrary"),
                                             vmem_limit_bytes=BIG_VMEM_LIMIT),
        name="out_proj",
    )(a, w, res)


def _narrow_rows(w_ref, valid):
    row = lax.broadcasted_iota(jnp.int32, (LANES, 1), 0)
    return jnp.where(row < valid, w_ref[...], 0.0).astype(BF16)


def _normed_rows(h_hbm, g_ref, n_ref, hbuf_ref, sem):
    i = pl.program_id(0)

    def h_copy(tile):
        return pltpu.make_async_copy(h_hbm.at[pl.ds(tile * PROJ_TM, PROJ_TM), :], hbuf_ref, sem.at[0])

    @pl.when(i == 0)
    def _():
        h_copy(0).start()

    h_copy(i).wait()
    n = _rms(hbuf_ref[...], g_ref[...]).astype(BF16)
    n_ref[...] = n

    @pl.when(i + 1 < pl.num_programs(0))
    def _():
        h_copy(i + 1).start()

    return n


PROJ_SCRATCH = [
    pltpu.VMEM((PROJ_TM, D_MODEL), BF16),
    pltpu.VMEM((PROJ_TM, D_MODEL), F32),
    pltpu.SemaphoreType.DMA((1,)),
]


def _ssd_in_kernel(h_hbm, g_ref, w_ref, wdt_ref, o_ref, dt_ref, n_ref, hbuf_ref, sem):
    @pl.when(pl.program_id(1) == 0)
    def _():
        n = _normed_rows(h_hbm, g_ref, n_ref, hbuf_ref, sem)
        dt_ref[...] = _dot_nt(n, _narrow_rows(wdt_ref, SSD_N_HEADS))

    o_ref[...] = _dot_nt(n_ref[...], w_ref[...].astype(BF16)).astype(o_ref.dtype)


def _ssd_in(h, mix_norm3, ssd_w_in_t, layer, j):
    t = h.shape[0]
    return pl.pallas_call(
        _ssd_in_kernel,
        grid=(t // PROJ_TM, SSD_MAIN_DIM // PROJ_TN),
        in_specs=[
            pl.BlockSpec(memory_space=pl.ANY),
            pl.BlockSpec((None, 1, D_MODEL), lambda i, c: (layer, 0, 0)),
            pl.BlockSpec((None, PROJ_TN, D_MODEL), lambda i, c: (j, c, 0)),
            pl.BlockSpec((None, LANES, D_MODEL), lambda i, c: (j, SSD_MAIN_DIM // LANES, 0)),
        ],
        out_specs=[
            pl.BlockSpec((PROJ_TM, PROJ_TN), lambda i, c: (i, c)),
            pl.BlockSpec((PROJ_TM, LANES), lambda i, c: (i, 0)),
        ],
        out_shape=[
            jax.ShapeDtypeStruct((t, SSD_MAIN_DIM), BF16),
            jax.ShapeDtypeStruct((t, LANES), F32),
        ],
        scratch_shapes=PROJ_SCRATCH,
        compiler_params=_params("arbitrary", "arbitrary"),
        name="ssd_in",
    )(h, mix_norm3, ssd_w_in_t, ssd_w_in_t)


def _conv_shift_matrix():
    q, blk = SSD_CHUNK, SSD_CHUNK + CONV_TAIL
    s = np.zeros((q, CONV_K), np.float32)
    for k in range(SSD_CONV - 1):
        shift = SSD_CONV - 1 - k
        for t in range(q):
            src = t - shift
            s[t, k * blk + (src if src >= 0 else q + CONV_TAIL + src)] = 1.0
    return jnp.asarray(s, BF16)


def _ssd_core_kernel(z_ref, xs_ref, bc_ref, dt_ref, cw_ref, cb_ref, dtb_ref, al_ref, dex_ref, nw_ref,
                     e_ref, e3_ref, sh_ref, o_ref, st_ref, tail_ref):
    q = SSD_CHUNK
    gw = SSD_GROUP_WIDTH
    ns = SSD_D_STATE

    @pl.when(pl.program_id(0) == 0)
    def _():
        st_ref[...] = jnp.zeros_like(st_ref)
        tail_ref[...] = jnp.zeros_like(tail_ref)

    row = lax.broadcasted_iota(jnp.int32, (q, q), 0)
    col = lax.broadcasted_iota(jnp.int32, (q, q), 1)
    tril = row >= col
    lower = jnp.where(tril, 1.0, 0.0).astype(BF16)
    lane = lax.broadcasted_iota(jnp.int32, (q, 2 * SSD_HEAD_DIM), 1)
    even_head = lane < SSD_HEAD_DIM

    for ci in range(SSD_CHUNKS_PER_STEP):
        r0 = ci * q

        def conv(src_ref, src_lo, lo, width):
            cur = src_ref[r0:r0 + q, src_lo:src_lo + width]
            if ci == 0:
                tail = tail_ref[:, lo:lo + width]
            else:
                tail = src_ref[r0 - CONV_TAIL:r0, src_lo:src_lo + width]
            parts = []
            for k in range(SSD_CONV - 1):
                wk = cw_ref[k:k + 1, lo:lo + width].astype(BF16)
                parts += [cur * wk, tail * wk]
            parts.append(jnp.zeros((CONV_K - (SSD_CONV - 1) * (q + CONV_TAIL), width), BF16))
            acc = _dot(sh_ref[...], jnp.concatenate(parts, axis=0))
            acc = acc + cur.astype(F32) * cw_ref[SSD_CONV - 1:SSD_CONV, lo:lo + width] + cb_ref[:, lo:lo + width]
            return _silu(acc)

        dt = _softplus(dt_ref[r0:r0 + q, :] + dtb_ref[...])
        da = dt * (-LOG2E * jnp.exp(al_ref[...]))
        acum = sum(_dot(lower, p) for p in _split3(da))
        acum_t = acum.T
        acum3 = jnp.concatenate(_split3(acum), axis=1)
        dt16 = dt.astype(BF16)

        for g in range(SSD_N_GROUPS):
            c0 = g * gw
            ae = _dot(acum3, e3_ref[:, c0:c0 + gw])
            dte = _dot(dt16, e_ref[:, c0:c0 + gw])
            xs = conv(xs_ref, c0, c0, gw)
            b0 = g * ns
            c1 = SSD_N_GROUPS * ns + g * ns
            bg = conv(bc_ref, b0, SSD_D_INNER + b0, ns).astype(BF16)
            cg = conv(bc_ref, c1, SSD_D_INNER + c1, ns).astype(BF16)
            cbm = _dot_nt(cg, bg)
            xdt = xs * dte

            st = st_ref[g]
            y = _dot(cg, st.astype(BF16)) * jnp.exp2(ae)

            pieces = []
            for pr in range(SSD_HEADS_PER_GROUP // 2):
                def scores(hh):
                    seg = acum[:, hh:hh + 1] - acum_t[hh:hh + 1, :]
                    return (cbm * jnp.exp2(jnp.where(tril, seg, -jnp.inf))).astype(BF16)

                h0 = g * SSD_HEADS_PER_GROUP + 2 * pr
                lhs = jnp.concatenate([scores(h0), scores(h0 + 1)], axis=1)
                l0 = pr * 2 * SSD_HEAD_DIM
                xp = xdt[:, l0:l0 + 2 * SSD_HEAD_DIM]
                rhs = jnp.concatenate([jnp.where(even_head, xp, 0.0),
                                       jnp.where(even_head, 0.0, xp)], axis=0)
                pieces.append(_dot(lhs, rhs.astype(BF16)))
            y = y + jnp.concatenate(pieces, axis=1)

            last = ae[q - 1:q, :]
            to_end = jnp.exp2(last - ae)
            st_ref[g] = st * jnp.exp2(last) + _dot_tn(bg, (xdt * to_end).astype(BF16))

            y = y + dex_ref[:, c0:c0 + gw] * xs
            gated = y * _silu(z_ref[r0:r0 + q, c0:c0 + gw].astype(F32))
            o_ref[r0:r0 + q, c0:c0 + gw] = _rms(gated, nw_ref[:, c0:c0 + gw]).astype(o_ref.dtype)

    rows = SSD_CHUNKS_PER_STEP * q
    tail_ref[:, 0:SSD_D_INNER] = xs_ref[rows - CONV_TAIL:rows, :]
    tail_ref[:, SSD_D_INNER:SSD_CONV_DIM] = bc_ref[rows - CONV_TAIL:rows, :]


def _ssd_core(zx, dt, conv_w_t, conv_b, dt_bias, a_log, d_exp, norm_w, expand):
    t = zx.shape[0]
    q = SSD_CHUNK
    rows = SSD_CHUNKS_PER_STEP * q
    full = lambda shape: pl.BlockSpec(shape, lambda c: (0,) * len(shape))
    pad_heads = lambda v: jnp.pad(v, (0, LANES - SSD_N_HEADS)).reshape(1, LANES)
    return pl.pallas_call(
        _ssd_core_kernel,
        grid=(t // rows,),
        in_specs=[
            pl.BlockSpec((rows, SSD_D_INNER), lambda c: (c, 0)),
            pl.BlockSpec((rows, SSD_D_INNER), lambda c: (c, 1)),
            pl.BlockSpec((rows, 2 * SSD_N_GROUPS * SSD_D_STATE), lambda c: (c, 4)),
            pl.BlockSpec((rows, LANES), lambda c: (c, 0)),
            full((SSD_CONV, SSD_CONV_DIM)),
            full((1, SSD_CONV_DIM)),
            full((1, LANES)),
            full((1, LANES)),
            full((1, SSD_D_INNER)),
            full((1, SSD_D_INNER)),
            full((LANES, SSD_D_INNER)),
            full((3 * LANES, SSD_D_INNER)),
            full((q, CONV_K)),
        ],
        out_specs=pl.BlockSpec((rows, SSD_D_INNER), lambda c: (c, 0)),
        out_shape=jax.ShapeDtypeStruct((t, SSD_D_INNER), BF16),
        scratch_shapes=[
            pltpu.VMEM((SSD_N_GROUPS, SSD_D_STATE, SSD_GROUP_WIDTH), F32),
            pltpu.VMEM((CONV_TAIL, SSD_CONV_DIM), BF16),
        ],
        compiler_params=_params("arbitrary"),
        name="ssd_core",
    )(zx, zx, zx, dt, conv_w_t, conv_b, pad_heads(dt_bias), pad_heads(a_log), d_exp, norm_w,
      expand, jnp.tile(expand, (3, 1)), _conv_shift_matrix())


def _ssd_mixer(h, mix_norm3, layer, j, ssd_w_in, ssd_conv_w, ssd_conv_b, ssd_dt_bias, ssd_a_log,
               ssd_d, ssd_norm, ssd_w_out):
    zx, dt = _ssd_in(h, mix_norm3, jnp.swapaxes(ssd_w_in, 1, 2), layer, j)
    head_of_channel = jnp.arange(SSD_D_INNER) // SSD_HEAD_DIM
    expand = (jnp.arange(LANES)[:, None] == head_of_channel[None, :]).astype(BF16)
    d_exp = jnp.repeat(ssd_d[j], SSD_HEAD_DIM).reshape(1, SSD_D_INNER)
    yn = _ssd_core(zx, dt, ssd_conv_w[j].T, ssd_conv_b[j].reshape(1, -1), ssd_dt_bias[j],
                   ssd_a_log[j], d_exp, ssd_norm[j].reshape(1, -1), expand)
    return _mm_res(yn, ssd_w_out, j, h)


def _gla_in_kernel(h_hbm, g_ref, w_ref, wl_ref, o_ref, gl_ref, n_ref, hbuf_ref, sem):
    @pl.when(pl.program_id(1) == 0)
    def _():
        n = _normed_rows(h_hbm, g_ref, n_ref, hbuf_ref, sem)
        gl_ref[...] = _dot_nt(n, _narrow_rows(wl_ref, GLA_GATE_RANK))

    o_ref[...] = _dot_nt(n_ref[...], w_ref[...].astype(BF16)).astype(o_ref.dtype)


def _gla_in(h, mix_norm3, gla_w_in_t, layer, j):
    t = h.shape[0]
    return pl.pallas_call(
        _gla_in_kernel,
        grid=(t // PROJ_TM, GLA_MAIN_DIM // PROJ_TN),
        in_specs=[
            pl.BlockSpec(memory_space=pl.ANY),
            pl.BlockSpec((None, 1, D_MODEL), lambda i, c: (layer, 0, 0)),
            pl.BlockSpec((None, PROJ_TN, D_MODEL), lambda i, c: (j, c, 0)),
            pl.BlockSpec((None, LANES, D_MODEL), lambda i, c: (j, GLA_MAIN_DIM // LANES, 0)),
        ],
        out_specs=[
            pl.BlockSpec((PROJ_TM, PROJ_TN), lambda i, c: (i, c)),
            pl.BlockSpec((PROJ_TM, LANES), lambda i, c: (i, 0)),
        ],
        out_shape=[
            jax.ShapeDtypeStruct((t, GLA_MAIN_DIM), BF16),
            jax.ShapeDtypeStruct((t, LANES), F32),
        ],
        scratch_shapes=PROJ_SCRATCH,
        compiler_params=_params("arbitrary", "arbitrary"),
        name="gla_in",
    )(h, mix_norm3, gla_w_in_t, gla_w_in_t)


def _gla_core_kernel(x_ref, gl_ref, w2_ref, bg_ref, nw_ref, o_ref, st_ref):
    q = GLA_CHUNK
    hk, hv = GLA_HEAD_K, GLA_HEAD_V

    @pl.when(pl.program_id(0) == 0)
    def _():
        st_ref[...] = jnp.zeros_like(st_ref)

    row = lax.broadcasted_iota(jnp.int32, (q, q), 0)
    col = lax.broadcasted_iota(jnp.int32, (q, q), 1)
    tril = row >= col
    lower = jnp.where(tril, 1.0, 0.0).astype(BF16)
    w2 = w2_ref[...].astype(BF16)

    for ci in range(GLA_CHUNKS_PER_STEP):
        r0 = ci * q
        lg = _dot(gl_ref[r0:r0 + q, :].astype(BF16), w2) + bg_ref[...]
        log_a = (jnp.minimum(lg, 0.0) - jnp.log1p(jnp.exp(-jnp.abs(lg)))) * (LOG2E / GLA_GATE_TAU)
        bcum = sum(_dot(lower, p) for p in _split3(log_a))
        for hd in range(GLA_N_HEADS):
            k0 = hd * hk
            v0 = 2 * GLA_D_K + hd * hv
            qh = x_ref[r0:r0 + q, k0:k0 + hk].astype(F32) * (hk ** -0.5)
            kh = x_ref[r0:r0 + q, GLA_D_K + k0:GLA_D_K + k0 + hk].astype(F32)
            vh = x_ref[r0:r0 + q, v0:v0 + hv]
            rh = x_ref[r0:r0 + q, v0 + GLA_D_V:v0 + GLA_D_V + hv].astype(F32)
            bc = bcum[:, k0:k0 + hk]
            mid = bc[q // 2:q // 2 + 1, :]
            end = bc[q - 1:q, :]
            att = _dot_nt((qh * jnp.exp2(bc - mid)).astype(BF16), (kh * jnp.exp2(mid - bc)).astype(BF16))
            att = jnp.where(tril, att, 0.0)
            o = _dot(att.astype(BF16), vh)
            st = st_ref[hd]
            o = o + _dot_nt((qh * jnp.exp2(bc)).astype(BF16), st.astype(BF16))
            k_out = (kh * jnp.exp2(end - bc)).astype(BF16)
            st_ref[hd] = st * jnp.exp2(end) + _dot_tn(vh, k_out)
            o = _rms(o, nw_ref[...]) * _silu(rh)
            o_ref[r0:r0 + q, hd * hv:(hd + 1) * hv] = o.astype(o_ref.dtype)


def _gla_core(qkvr, g_low, w_gate2, b_gate, norm_w):
    t = qkvr.shape[0]
    rows = GLA_CHUNK * GLA_CHUNKS_PER_STEP
    return pl.pallas_call(
        _gla_core_kernel,
        grid=(t // rows,),
        in_specs=[
            pl.BlockSpec((rows, GLA_MAIN_DIM), lambda c: (c, 0)),
            pl.BlockSpec((rows, LANES), lambda c: (c, 0)),
            pl.BlockSpec((LANES, GLA_D_K), lambda c: (0, 0)),
            pl.BlockSpec((1, GLA_D_K), lambda c: (0, 0)),
            pl.BlockSpec((1, GLA_HEAD_V), lambda c: (0, 0)),
        ],
        out_specs=pl.BlockSpec((rows, GLA_D_V), lambda c: (c, 0)),
        out_shape=jax.ShapeDtypeStruct((t, GLA_D_V), BF16),
        scratch_shapes=[pltpu.VMEM((GLA_N_HEADS, GLA_HEAD_V, GLA_HEAD_K), F32)],
        compiler_params=_params("arbitrary"),
        name="gla_core",
    )(qkvr, g_low, w_gate2, b_gate, norm_w)


def _gla_mixer(h, mix_norm3, layer, j, gla_w_in, gla_w_gate2, gla_b_gate, gla_norm, gla_w_out):
    qkvr, g_low = _gla_in(h, mix_norm3, jnp.swapaxes(gla_w_in, 1, 2), layer, j)
    w_gate2 = jnp.pad(gla_w_gate2[j], ((0, LANES - GLA_GATE_RANK), (0, 0)))
    o = _gla_core(qkvr, g_low, w_gate2, gla_b_gate[j].reshape(1, -1), gla_norm[j].reshape(1, -1))
    return _mm_res(o, gla_w_out, j, h)


def _sgu_in_kernel(h_hbm, g_ref, w_ref, b_ref, o_ref, n_ref, hbuf_ref, sem):
    @pl.when(pl.program_id(1) == 0)
    def _():
        _normed_rows(h_hbm, g_ref, n_ref, hbuf_ref, sem)

    y = _dot(n_ref[...], w_ref[...].astype(BF16)) + b_ref[...]
    o_ref[...] = (0.5 * y * (1.0 + lax.erf(y * (0.5 ** 0.5)))).astype(o_ref.dtype)


def _sgu_in(h, mix_norm3, sgu_w_in, sgu_b_in3, layer, j):
    t = h.shape[0]
    return pl.pallas_call(
        _sgu_in_kernel,
        grid=(t // PROJ_TM, 2 * SGU_WIDTH // PROJ_TN),
        in_specs=[
            pl.BlockSpec(memory_space=pl.ANY),
            pl.BlockSpec((None, 1, D_MODEL), lambda i, c: (layer, 0, 0)),
            pl.BlockSpec((None, D_MODEL, PROJ_TN), lambda i, c: (j, 0, c)),
            pl.BlockSpec((None, 1, PROJ_TN), lambda i, c: (j, 0, c)),
        ],
        out_specs=pl.BlockSpec((PROJ_TM, PROJ_TN), lambda i, c: (i, c)),
        out_shape=jax.ShapeDtypeStruct((t, 2 * SGU_WIDTH), BF16),
        scratch_shapes=PROJ_SCRATCH,
        compiler_params=_params("arbitrary", "arbitrary"),
        name="sgu_in",
    )(h, mix_norm3, sgu_w_in, sgu_b_in3)


def _sgu_core_kernel(u_ref, v_ref, nw_ref, ws_ref, bs_ref, o_ref):
    q = SGU_CHUNK
    gd = SGU_GROUP_DIM
    row = lax.broadcasted_iota(jnp.int32, (q, q), 0)
    col = lax.broadcasted_iota(jnp.int32, (q, q), 1)
    tril = row >= col
    for ci in range(SGU_CHUNKS_PER_STEP):
        r0 = ci * q
        vn = _rms(v_ref[r0:r0 + q, :].astype(F32), nw_ref[...]).astype(BF16)
        for g in range(SGU_N_GROUPS):
            c0 = g * gd
            wc = jnp.where(tril, ws_ref[g], 0.0).astype(BF16)
            sv = _dot(wc, vn[:, c0:c0 + gd]) + bs_ref[:, g:g + 1]
            o_ref[r0:r0 + q, c0:c0 + gd] = (u_ref[r0:r0 + q, c0:c0 + gd].astype(F32) * sv).astype(o_ref.dtype)


def _sgu_core(zz, norm_w, w_s, b_s_t):
    t = zz.shape[0]
    rows = SGU_CHUNK * SGU_CHUNKS_PER_STEP
    return pl.pallas_call(
        _sgu_core_kernel,
        grid=(t // rows,),
        in_specs=[
            pl.BlockSpec((rows, SGU_WIDTH), lambda c: (c, 0)),
            pl.BlockSpec((rows, SGU_WIDTH), lambda c: (c, 1)),
            pl.BlockSpec((1, SGU_WIDTH), lambda c: (0, 0)),
            pl.BlockSpec((SGU_N_GROUPS, SGU_CHUNK, SGU_CHUNK), lambda c: (0, 0, 0)),
            pl.BlockSpec((SGU_CHUNK, SGU_N_GROUPS), lambda c: (0, 0)),
        ],
        out_specs=pl.BlockSpec((rows, SGU_WIDTH), lambda c: (c, 0)),
        out_shape=jax.ShapeDtypeStruct((t, SGU_WIDTH), BF16),
        compiler_params=_params("parallel"),
        name="sgu_core",
    )(zz, zz, norm_w, w_s, b_s_t)


def _sgu_mixer(h, mix_norm3, layer, j, sgu_w_in, sgu_b_in, sgu_norm, sgu_w_s, sgu_b_s, sgu_w_out):
    zz = _sgu_in(h, mix_norm3, sgu_w_in, sgu_b_in.reshape(sgu_b_in.shape[0], 1, -1), layer, j)
    y = _sgu_core(zz, sgu_norm[j].reshape(1, -1), sgu_w_s[j], sgu_b_s[j].T)
    return _mm_res(y, sgu_w_out, j, h)


def kernel(x, ffn_norm, ffn_w_in, ffn_w_out, mix_norm, ssd_w_in, ssd_conv_w, ssd_conv_b, ssd_dt_bias,
           ssd_a_log, ssd_d, ssd_norm, ssd_w_out, gla_w_in, gla_w_gate2, gla_b_gate, gla_norm, gla_w_out,
           sgu_w_in, sgu_b_in, sgu_norm, sgu_w_s, sgu_b_s, sgu_w_out, final_norm):
    b, t, d = x.shape
    assert (b, t, d) == (1, SEQ, D_MODEL)
    h = x.reshape(t, d)
    ffn_norm4 = ffn_norm.reshape(DEPTH, 2, 1, D_MODEL)
    mix_norm3 = mix_norm.reshape(DEPTH, 1, D_MODEL)
    final_w = final_norm.reshape(1, D_MODEL)
    for i in range(DEPTH):
        h = _ffn(h, ffn_norm4, ffn_w_in, ffn_w_out, final_w, i, 0, False)
        kind, j = i % N_MIXERS, i // N_MIXERS
        if kind == 0:
            h = _ssd_mixer(h, mix_norm3, i, j, ssd_w_in, ssd_conv_w, ssd_conv_b, ssd_dt_bias, ssd_a_log,
                           ssd_d, ssd_norm, ssd_w_out)
        elif kind == 1:
            h = _gla_mixer(h, mix_norm3, i, j, gla_w_in, gla_w_gate2, gla_b_gate, gla_norm, gla_w_out)
        else:
            h = _sgu_mixer(h, mix_norm3, i, j, sgu_w_in, sgu_b_in, sgu_norm, sgu_w_s, sgu_b_s, sgu_w_out)
        h = _ffn(h, ffn_norm4, ffn_w_in, ffn_w_out, final_w, i, 1, i == DEPTH - 1)
    return h.reshape(b, t, d)
```

```python
import functools

import jax
import jax.numpy as jnp
import numpy as np
from jax import lax
from jax.experimental import pallas as pl
from jax.experimental.pallas import tpu as pltpu

F32 = jnp.float32
BF16 = jnp.bfloat16

D_MODEL = 2048
SEQ = 8192
DEPTH = 4
N_MIXERS = 3
NORM_EPS = 1e-6
D_FF = 5632

SSD_D_INNER = 4096
SSD_HEAD_DIM = 64
SSD_N_HEADS = 64
SSD_N_GROUPS = 8
SSD_HEADS_PER_GROUP = 8
SSD_D_STATE = 128
SSD_CONV = 4
SSD_CHUNK = 128
SSD_CONV_DIM = 6144
SSD_GROUP_WIDTH = SSD_HEADS_PER_GROUP * SSD_HEAD_DIM
SSD_MAIN_DIM = SSD_D_INNER + SSD_CONV_DIM

GLA_N_HEADS = 4
GLA_D_K = 1024
GLA_D_V = 2048
GLA_HEAD_K = 256
GLA_HEAD_V = 512
GLA_GATE_RANK = 16
GLA_GATE_TAU = 16.0
GLA_CHUNK = 64
GLA_MAIN_DIM = 2 * GLA_D_K + 2 * GLA_D_V

SGU_WIDTH = 4096
SGU_N_GROUPS = 8
SGU_GROUP_DIM = 512
SGU_CHUNK = 128

FFN_TM = 2048
FFN_TF = 256
PROJ_TM = 2048
PROJ_TN = 512
OUT_TM = 512
OUT_TN = 1024
SSD_CHUNKS_PER_STEP = 2
GLA_CHUNKS_PER_STEP = 4
SGU_CHUNKS_PER_STEP = 4
LANES = 128
BF16_ROWS = 16
CONV_TAIL = BF16_ROWS
CONV_K = -(-(SSD_CONV - 1) * (SSD_CHUNK + CONV_TAIL) // LANES) * LANES
LOG2E = 1.4426950408889634

VMEM_LIMIT = 56 * 1024 * 1024
BIG_VMEM_LIMIT = 62 * 1024 * 1024


def _silu(x):
    hx = 0.5 * x
    return hx * jnp.tanh(hx) + hx


def _softplus(x):
    return jnp.maximum(x, 0.0) + jnp.log1p(jnp.exp(-jnp.abs(x)))


def _rms(x, w):
    return x * lax.rsqrt(jnp.mean(x * x, axis=-1, keepdims=True) + NORM_EPS) * w


def _dot(a, b):
    return jnp.dot(a, b, preferred_element_type=F32)


def _dot_nt(a, b):
    return lax.dot_general(a, b, (((1,), (1,)), ((), ())), preferred_element_type=F32)


def _dot_tn(a, b):
    return lax.dot_general(a, b, (((0,), (0,)), ((), ())), preferred_element_type=F32)


def _split3(x):
    x1 = x.astype(BF16)
    r1 = x - x1.astype(F32)
    x2 = r1.astype(BF16)
    r2 = r1 - x2.astype(F32)
    return x1, x2, r2.astype(BF16)


def _params(*sem):
    return pltpu.CompilerParams(dimension_semantics=sem, vmem_limit_bytes=VMEM_LIMIT)


def _ffn_kernel(h_hbm, g_ref, wg_ref, wu_ref, wo_ref, fw_ref, o_hbm, n_ref, acc_ref, in_sem, out_sem, *, final):
    i = pl.program_id(0)
    j = pl.program_id(1)
    n_tiles = pl.num_programs(0)
    slot = i % 2
    other = 1 - slot

    def h_copy(tile, s):
        return pltpu.make_async_copy(h_hbm.at[pl.ds(tile * FFN_TM, FFN_TM), :], acc_ref.at[s], in_sem.at[s])

    def o_copy(tile, s):
        return pltpu.make_async_copy(acc_ref.at[s], o_hbm.at[pl.ds(tile * FFN_TM, FFN_TM), :], out_sem.at[s])

    @pl.when(j == 0)
    def _():
        @pl.when(i == 0)
        def _():
            h_copy(0, 0).start()

        h_copy(i, slot).wait()
        n_ref[...] = _rms(acc_ref[slot], g_ref[...]).astype(BF16)

    @pl.when(jnp.logical_and(j == 1, i + 1 < n_tiles))
    def _():
        @pl.when(i >= 1)
        def _():
            o_copy(i - 1, other).wait()

        h_copy(i + 1, other).start()

    n = n_ref[...]
    gate = _dot(n, wg_ref[...].astype(BF16))
    up = _dot(n, wu_ref[...].astype(BF16))
    act = (0.5 * _silu(gate) * up).astype(BF16)
    acc_ref[slot] += _dot(act, wo_ref[...].astype(BF16))

    @pl.when(j == pl.num_programs(1) - 1)
    def _():
        if final:
            acc_ref[slot] = _rms(acc_ref[slot], fw_ref[...])
        o_copy(i, slot).start()

        @pl.when(i == n_tiles - 1)
        def _():
            o_copy(i, slot).wait()

            @pl.when(i >= 1)
            def _():
                o_copy(i - 1, other).wait()


def _ffn(h, ffn_norm4, ffn_w_in, ffn_w_out, final_w, layer, half, final):
    t = h.shape[0]
    nf = D_FF // FFN_TF
    return pl.pallas_call(
        functools.partial(_ffn_kernel, final=final),
        grid=(t // FFN_TM, nf),
        in_specs=[
            pl.BlockSpec(memory_space=pl.ANY),
            pl.BlockSpec((None, None, 1, D_MODEL), lambda i, j: (layer, half, 0, 0)),
            pl.BlockSpec((None, None, D_MODEL, FFN_TF), lambda i, j: (layer, half, 0, j)),
            pl.BlockSpec((None, None, D_MODEL, FFN_TF), lambda i, j: (layer, half, 0, j + nf)),
            pl.BlockSpec((None, None, FFN_TF, D_MODEL), lambda i, j: (layer, half, j, 0)),
            pl.BlockSpec((1, D_MODEL), lambda i, j: (0, 0)),
        ],
        out_specs=pl.BlockSpec(memory_space=pl.ANY),
        out_shape=jax.ShapeDtypeStruct((t, D_MODEL), F32),
        scratch_shapes=[
            pltpu.VMEM((FFN_TM, D_MODEL), BF16),
            pltpu.VMEM((2, FFN_TM, D_MODEL), F32),
            pltpu.SemaphoreType.DMA((2,)),
            pltpu.SemaphoreType.DMA((2,)),
        ],
        compiler_params=pltpu.CompilerParams(dimension_semantics=("arbitrary", "arbitrary"),
                                             vmem_limit_bytes=BIG_VMEM_LIMIT),
        name="ffn",
    )(h, ffn_norm4, ffn_w_in, ffn_w_in, ffn_w_out, final_w)


def _mm_res_kernel(a_ref, w_ref, r_ref, o_ref, wb_ref):
    @pl.when(pl.program_id(1) == 0)
    def _():
        wb_ref[...] = w_ref[...].astype(BF16)

    o_ref[...] = r_ref[...] + _dot(a_ref[...], wb_ref[...])


def _mm_res(a, w, layer, res):
    t, k = a.shape
    return pl.pallas_call(
        _mm_res_kernel,
        grid=(D_MODEL // OUT_TN, t // OUT_TM),
        in_specs=[
            pl.BlockSpec((OUT_TM, k), lambda j, i: (i, 0)),
            pl.BlockSpec((None, k, OUT_TN), lambda j, i: (layer, 0, j)),
            pl.BlockSpec((OUT_TM, OUT_TN), lambda j, i: (i, j)),
        ],
        out_specs=pl.BlockSpec((OUT_TM, OUT_TN), lambda j, i: (i, j)),
        out_shape=jax.ShapeDtypeStruct((t, D_MODEL), F32),
        scratch_shapes=[pltpu.VMEM((k, OUT_TN), BF16)],
        compiler_params=pltpu.CompilerParams(dimension_semantics=("parallel", "arbitrary"),
                                             vmem_limit_bytes=BIG_VMEM_LIMIT),
        name="out_proj",
    )(a, w, res)


def _narrow_rows(w_ref, valid):
    row = lax.broadcasted_iota(jnp.int32, (LANES, 1), 0)
    return jnp.where(row < valid, w_ref[...], 0.0).astype(BF16)


def _normed_rows(h_hbm, g_ref, n_ref, hbuf_ref, sem):
    i = pl.program_id(0)

    def h_copy(tile):
        return pltpu.make_async_copy(h_hbm.at[pl.ds(tile * PROJ_TM, PROJ_TM), :], hbuf_ref, sem.at[0])

    @pl.when(i == 0)
    def _():
        h_copy(0).start()

    h_copy(i).wait()
    n = _rms(hbuf_ref[...], g_ref[...]).astype(BF16)
    n_ref[...] = n

    @pl.when(i + 1 < pl.num_programs(0))
    def _():
        h_copy(i + 1).start()

    return n


PROJ_SCRATCH = [
    pltpu.VMEM((PROJ_TM, D_MODEL), BF16),
    pltpu.VMEM((PROJ_TM, D_MODEL), F32),
    pltpu.SemaphoreType.DMA((1,)),
]


def _ssd_in_kernel(h_hbm, g_ref, w_ref, wdt_ref, o_ref, dt_ref, n_ref, hbuf_ref, sem):
    @pl.when(pl.program_id(1) == 0)
    def _():
        n = _normed_rows(h_hbm, g_ref, n_ref, hbuf_ref, sem)
        dt_ref[...] = _dot_nt(n, _narrow_rows(wdt_ref, SSD_N_HEADS))

    o_ref[...] = _dot_nt(n_ref[...], w_ref[...].astype(BF16)).astype(o_ref.dtype)


def _ssd_in(h, mix_norm3, ssd_w_in_t, layer, j):
    t = h.shape[0]
    return pl.pallas_call(
        _ssd_in_kernel,
        grid=(t // PROJ_TM, SSD_MAIN_DIM // PROJ_TN),
        in_specs=[
            pl.BlockSpec(memory_space=pl.ANY),
            pl.BlockSpec((None, 1, D_MODEL), lambda i, c: (layer, 0, 0)),
            pl.BlockSpec((None, PROJ_TN, D_MODEL), lambda i, c: (j, c, 0)),
            pl.BlockSpec((None, LANES, D_MODEL), lambda i, c: (j, SSD_MAIN_DIM // LANES, 0)),
        ],
        out_specs=[
            pl.BlockSpec((PROJ_TM, PROJ_TN), lambda i, c: (i, c)),
            pl.BlockSpec((PROJ_TM, LANES), lambda i, c: (i, 0)),
        ],
        out_shape=[
            jax.ShapeDtypeStruct((t, SSD_MAIN_DIM), BF16),
            jax.ShapeDtypeStruct((t, LANES), F32),
        ],
        scratch_shapes=PROJ_SCRATCH,
        compiler_params=_params("arbitrary", "arbitrary"),
        name="ssd_in",
    )(h, mix_norm3, ssd_w_in_t, ssd_w_in_t)


def _conv_shift_matrix():
    q, blk = SSD_CHUNK, SSD_CHUNK + CONV_TAIL
    s = np.zeros((q, CONV_K), np.float32)
    for k in range(SSD_CONV - 1):
        shift = SSD_CONV - 1 - k
        for t in range(q):
            src = t - shift
            s[t, k * blk + (src if src >= 0 else q + CONV_TAIL + src)] = 1.0
    return jnp.asarray(s, BF16)


def _ssd_core_kernel(z_ref, xs_ref, bc_ref, dt_ref, cw_ref, cb_ref, dtb_ref, al_ref, dex_ref, nw_ref,
                     e_ref, e3_ref, sh_ref, o_ref, st_ref, tail_ref):
    q = SSD_CHUNK
    gw = SSD_GROUP_WIDTH
    ns = SSD_D_STATE

    @pl.when(pl.program_id(0) == 0)
    def _():
        st_ref[...] = jnp.zeros_like(st_ref)
        tail_ref[...] = jnp.zeros_like(tail_ref)

    row = lax.broadcasted_iota(jnp.int32, (q, q), 0)
    col = lax.broadcasted_iota(jnp.int32, (q, q), 1)
    tril = row >= col
    lower = jnp.where(tril, 1.0, 0.0).astype(BF16)
    lane = lax.broadcasted_iota(jnp.int32, (q, 2 * SSD_HEAD_DIM), 1)
    even_head = lane < SSD_HEAD_DIM

    for ci in range(SSD_CHUNKS_PER_STEP):
        r0 = ci * q

        def conv(src_ref, src_lo, lo, width):
            cur = src_ref[r0:r0 + q, src_lo:src_lo + width]
            if ci == 0:
                tail = tail_ref[:, lo:lo + width]
            else:
                tail = src_ref[r0 - CONV_TAIL:r0, src_lo:src_lo + width]
            parts = []
            for k in range(SSD_CONV - 1):
                wk = cw_ref[k:k + 1, lo:lo + width].astype(BF16)
                parts += [cur * wk, tail * wk]
            parts.append(jnp.zeros((CONV_K - (SSD_CONV - 1) * (q + CONV_TAIL), width), BF16))
            acc = _dot(sh_ref[...], jnp.concatenate(parts, axis=0))
            acc = acc + cur.astype(F32) * cw_ref[SSD_CONV - 1:SSD_CONV, lo:lo + width] + cb_ref[:, lo:lo + width]
            return _silu(acc)

        dt = _softplus(dt_ref[r0:r0 + q, :] + dtb_ref[...])
        da = dt * (-LOG2E * jnp.exp(al_ref[...]))
        acum = sum(_dot(lower, p) for p in _split3(da))
        acum_t = acum.T
        acum3 = jnp.concatenate(_split3(acum), axis=1)
        dt16 = dt.astype(BF16)

        for g in range(SSD_N_GROUPS):
            c0 = g * gw
            ae = _dot(acum3, e3_ref[:, c0:c0 + gw])
            dte = _dot(dt16, e_ref[:, c0:c0 + gw])
            xs = conv(xs_ref, c0, c0, gw)
            b0 = g * ns
            c1 = SSD_N_GROUPS * ns + g * ns
            bg = conv(bc_ref, b0, SSD_D_INNER + b0, ns).astype(BF16)
            cg = conv(bc_ref, c1, SSD_D_INNER + c1, ns).astype(BF16)
            cbm = _dot_nt(cg, bg)
            xdt = xs * dte

            st = st_ref[g]
            y = _dot(cg, st.astype(BF16)) * jnp.exp2(ae)

            pieces = []
            for pr in range(SSD_HEADS_PER_GROUP // 2):
                def scores(hh):
                    seg = acum[:, hh:hh + 1] - acum_t[hh:hh + 1, :]
                    return (cbm * jnp.exp2(jnp.where(tril, seg, -jnp.inf))).astype(BF16)

                h0 = g * SSD_HEADS_PER_GROUP + 2 * pr
                lhs = jnp.concatenate([scores(h0), scores(h0 + 1)], axis=1)
                l0 = pr * 2 * SSD_HEAD_DIM
                xp = xdt[:, l0:l0 + 2 * SSD_HEAD_DIM]
                rhs = jnp.concatenate([jnp.where(even_head, xp, 0.0),
                                       jnp.where(even_head, 0.0, xp)], axis=0)
                pieces.append(_dot(lhs, rhs.astype(BF16)))
            y = y + jnp.concatenate(pieces, axis=1)

            last = ae[q - 1:q, :]
            to_end = jnp.exp2(last - ae)
            st_ref[g] = st * jnp.exp2(last) + _dot_tn(bg, (xdt * to_end).astype(BF16))

            y = y + dex_ref[:, c0:c0 + gw] * xs
            gated = y * _silu(z_ref[r0:r0 + q, c0:c0 + gw].astype(F32))
            o_ref[r0:r0 + q, c0:c0 + gw] = _rms(gated, nw_ref[:, c0:c0 + gw]).astype(o_ref.dtype)

    rows = SSD_CHUNKS_PER_STEP * q
    tail_ref[:, 0:SSD_D_INNER] = xs_ref[rows - CONV_TAIL:rows, :]
    tail_ref[:, SSD_D_INNER:SSD_CONV_DIM] = bc_ref[rows - CONV_TAIL:rows, :]


def _ssd_core(zx, dt, conv_w_t, conv_b, dt_bias, a_log, d_exp, norm_w, expand):
    t = zx.shape[0]
    q = SSD_CHUNK
    rows = SSD_CHUNKS_PER_STEP * q
    full = lambda shape: pl.BlockSpec(shape, lambda c: (0,) * len(shape))
    pad_heads = lambda v: jnp.pad(v, (0, LANES - SSD_N_HEADS)).reshape(1, LANES)
    return pl.pallas_call(
        _ssd_core_kernel,
        grid=(t // rows,),
        in_specs=[
            pl.BlockSpec((rows, SSD_D_INNER), lambda c: (c, 0)),
            pl.BlockSpec((rows, SSD_D_INNER), lambda c: (c, 1)),
            pl.BlockSpec((rows, 2 * SSD_N_GROUPS * SSD_D_STATE), lambda c: (c, 4)),
            pl.BlockSpec((rows, LANES), lambda c: (c, 0)),
            full((SSD_CONV, SSD_CONV_DIM)),
            full((1, SSD_CONV_DIM)),
            full((1, LANES)),
            full((1, LANES)),
            full((1, SSD_D_INNER)),
            full((1, SSD_D_INNER)),
            full((LANES, SSD_D_INNER)),
            full((3 * LANES, SSD_D_INNER)),
            full((q, CONV_K)),
        ],
        out_specs=pl.BlockSpec((rows, SSD_D_INNER), lambda c: (c, 0)),
        out_shape=jax.ShapeDtypeStruct((t, SSD_D_INNER), BF16),
        scratch_shapes=[
            pltpu.VMEM((SSD_N_GROUPS, SSD_D_STATE, SSD_GROUP_WIDTH), F32),
            pltpu.VMEM((CONV_TAIL, SSD_CONV_DIM), BF16),
        ],
        compiler_params=_params("arbitrary"),
        name="ssd_core",
    )(zx, zx, zx, dt, conv_w_t, conv_b, pad_heads(dt_bias), pad_heads(a_log), d_exp, norm_w,
      expand, jnp.tile(expand, (3, 1)), _conv_shift_matrix())


def _ssd_mixer(h, mix_norm3, layer, j, ssd_w_in, ssd_conv_w, ssd_conv_b, ssd_dt_bias, ssd_a_log,
               ssd_d, ssd_norm, ssd_w_out):
    zx, dt = _ssd_in(h, mix_norm3, jnp.swapaxes(ssd_w_in, 1, 2), layer, j)
    head_of_channel = jnp.arange(SSD_D_INNER) // SSD_HEAD_DIM
    expand = (jnp.arange(LANES)[:, None] == head_of_channel[None, :]).astype(BF16)
    d_exp = jnp.repeat(ssd_d[j], SSD_HEAD_DIM).reshape(1, SSD_D_INNER)
    yn = _ssd_core(zx, dt, ssd_conv_w[j].T, ssd_conv_b[j].reshape(1, -1), ssd_dt_bias[j],
                   ssd_a_log[j], d_exp, ssd_norm[j].reshape(1, -1), expand)
    return _mm_res(yn, ssd_w_out, j, h)


def _gla_in_kernel(h_hbm, g_ref, w_ref, wl_ref, o_ref, gl_ref, n_ref, hbuf_ref, sem):
    @pl.when(pl.program_id(1) == 0)
    def _():
        n = _normed_rows(h_hbm, g_ref, n_ref, hbuf_ref, sem)
        gl_ref[...] = _dot_nt(n, _narrow_rows(wl_ref, GLA_GATE_RANK))

    o_ref[...] = _dot_nt(n_ref[...], w_ref[...].astype(BF16)).astype(o_ref.dtype)


def _gla_in(h, mix_norm3, gla_w_in_t, layer, j):
    t = h.shape[0]
    return pl.pallas_call(
        _gla_in_kernel,
        grid=(t // PROJ_TM, GLA_MAIN_DIM // PROJ_TN),
        in_specs=[
            pl.BlockSpec(memory_space=pl.ANY),
            pl.BlockSpec((None, 1, D_MODEL), lambda i, c: (layer, 0, 0)),
            pl.BlockSpec((None, PROJ_TN, D_MODEL), lambda i, c: (j, c, 0)),
            pl.BlockSpec((None, LANES, D_MODEL), lambda i, c: (j, GLA_MAIN_DIM // LANES, 0)),
        ],
        out_specs=[
            pl.BlockSpec((PROJ_TM, PROJ_TN), lambda i, c: (i, c)),
            pl.BlockSpec((PROJ_TM, LANES), lambda i, c: (i, 0)),
        ],
        out_shape=[
            jax.ShapeDtypeStruct((t, GLA_MAIN_DIM), BF16),
            jax.ShapeDtypeStruct((t, LANES), F32),
        ],
        scratch_shapes=PROJ_SCRATCH,
        compiler_params=_params("arbitrary", "arbitrary"),
        name="gla_in",
    )(h, mix_norm3, gla_w_in_t, gla_w_in_t)


def _gla_core_kernel(x_ref, gl_ref, w2_ref, bg_ref, nw_ref, o_ref, st_ref):
    q = GLA_CHUNK
    hk, hv = GLA_HEAD_K, GLA_HEAD_V

    @pl.when(pl.program_id(0) == 0)
    def _():
        st_ref[...] = jnp.zeros_like(st_ref)

    row = lax.broadcasted_iota(jnp.int32, (q, q), 0)
    col = lax.broadcasted_iota(jnp.int32, (q, q), 1)
    tril = row >= col
    lower = jnp.where(tril, 1.0, 0.0).astype(BF16)
    w2 = w2_ref[...].astype(BF16)

    for ci in range(GLA_CHUNKS_PER_STEP):
        r0 = ci * q
        lg = _dot(gl_ref[r0:r0 + q, :].astype(BF16), w2) + bg_ref[...]
        log_a = (jnp.minimum(lg, 0.0) - jnp.log1p(jnp.exp(-jnp.abs(lg)))) * (LOG2E / GLA_GATE_TAU)
        bcum = sum(_dot(lower, p) for p in _split3(log_a))
        for hd in range(GLA_N_HEADS):
            k0 = hd * hk
            v0 = 2 * GLA_D_K + hd * hv
            qh = x_ref[r0:r0 + q, k0:k0 + hk].astype(F32) * (hk ** -0.5)
            kh = x_ref[r0:r0 + q, GLA_D_K + k0:GLA_D_K + k0 + hk].astype(F32)
            vh = x_ref[r0:r0 + q, v0:v0 + hv]
            rh = x_ref[r0:r0 + q, v0 + GLA_D_V:v0 + GLA_D_V + hv].astype(F32)
            bc = bcum[:, k0:k0 + hk]
            mid = bc[q // 2:q // 2 + 1, :]
            end = bc[q - 1:q, :]
            att = _dot_nt((qh * jnp.exp2(bc - mid)).astype(BF16), (kh * jnp.exp2(mid - bc)).astype(BF16))
            att = jnp.where(tril, att, 0.0)
            o = _dot(att.astype(BF16), vh)
            st = st_ref[hd]
            o = o + _dot_nt((qh * jnp.exp2(bc)).astype(BF16), st.astype(BF16))
            k_out = (kh * jnp.exp2(end - bc)).astype(BF16)
            st_ref[hd] = st * jnp.exp2(end) + _dot_tn(vh, k_out)
            o = _rms(o, nw_ref[...]) * _silu(rh)
            o_ref[r0:r0 + q, hd * hv:(hd + 1) * hv] = o.astype(o_ref.dtype)


def _gla_core(qkvr, g_low, w_gate2, b_gate, norm_w):
    t = qkvr.shape[0]
    rows = GLA_CHUNK * GLA_CHUNKS_PER_STEP
    return pl.pallas_call(
        _gla_core_kernel,
        grid=(t // rows,),
        in_specs=[
            pl.BlockSpec((rows, GLA_MAIN_DIM), lambda c: (c, 0)),
            pl.BlockSpec((rows, LANES), lambda c: (c, 0)),
            pl.BlockSpec((LANES, GLA_D_K), lambda c: (0, 0)),
            pl.BlockSpec((1, GLA_D_K), lambda c: (0, 0)),
            pl.BlockSpec((1, GLA_HEAD_V), lambda c: (0, 0)),
        ],
        out_specs=pl.BlockSpec((rows, GLA_D_V), lambda c: (c, 0)),
        out_shape=jax.ShapeDtypeStruct((t, GLA_D_V), BF16),
        scratch_shapes=[pltpu.VMEM((GLA_N_HEADS, GLA_HEAD_V, GLA_HEAD_K), F32)],
        compiler_params=_params("arbitrary"),
        name="gla_core",
    )(qkvr, g_low, w_gate2, b_gate, norm_w)


def _gla_mixer(h, mix_norm3, layer, j, gla_w_in, gla_w_gate2, gla_b_gate, gla_norm, gla_w_out):
    qkvr, g_low = _gla_in(h, mix_norm3, jnp.swapaxes(gla_w_in, 1, 2), layer, j)
    w_gate2 = jnp.pad(gla_w_gate2[j], ((0, LANES - GLA_GATE_RANK), (0, 0)))
    o = _gla_core(qkvr, g_low, w_gate2, gla_b_gate[j].reshape(1, -1), gla_norm[j].reshape(1, -1))
    return _mm_res(o, gla_w_out, j, h)


def _sgu_in_kernel(h_hbm, g_ref, w_ref, b_ref, o_ref, n_ref, hbuf_ref, sem):
    @pl.when(pl.program_id(1) == 0)
    def _():
        _normed_rows(h_hbm, g_ref, n_ref, hbuf_ref, sem)

    y = _dot(n_ref[...], w_ref[...].astype(BF16)) + b_ref[...]
    o_ref[...] = (0.5 * y * (1.0 + lax.erf(y * (0.5 ** 0.5)))).astype(o_ref.dtype)


def _sgu_in(h, mix_norm3, sgu_w_in, sgu_b_in3, layer, j):
    t = h.shape[0]
    return pl.pallas_call(
        _sgu_in_kernel,
        grid=(t // PROJ_TM, 2 * SGU_WIDTH // PROJ_TN),
        in_specs=[
            pl.BlockSpec(memory_space=pl.ANY),
            pl.BlockSpec((None, 1, D_MODEL), lambda i, c: (layer, 0, 0)),
            pl.BlockSpec((None, D_MODEL, PROJ_TN), lambda i, c: (j, 0, c)),
            pl.BlockSpec((None, 1, PROJ_TN), lambda i, c: (j, 0, c)),
        ],
        out_specs=pl.BlockSpec((PROJ_TM, PROJ_TN), lambda i, c: (i, c)),
        out_shape=jax.ShapeDtypeStruct((t, 2 * SGU_WIDTH), BF16),
        scratch_shapes=PROJ_SCRATCH,
        compiler_params=_params("arbitrary", "arbitrary"),
        name="sgu_in",
    )(h, mix_norm3, sgu_w_in, sgu_b_in3)


def _sgu_core_kernel(u_ref, v_ref, nw_ref, ws_ref, bs_ref, o_ref):
    q = SGU_CHUNK
    gd = SGU_GROUP_DIM
    row = lax.broadcasted_iota(jnp.int32, (q, q), 0)
    col = lax.broadcasted_iota(jnp.int32, (q, q), 1)
    tril = row >= col
    for ci in range(SGU_CHUNKS_PER_STEP):
        r0 = ci * q
        vn = _rms(v_ref[r0:r0 + q, :].astype(F32), nw_ref[...]).astype(BF16)
        for g in range(SGU_N_GROUPS):
            c0 = g * gd
            wc = jnp.where(tril, ws_ref[g], 0.0).astype(BF16)
            sv = _dot(wc, vn[:, c0:c0 + gd]) + bs_ref[:, g:g + 1]
            o_ref[r0:r0 + q, c0:c0 + gd] = (u_ref[r0:r0 + q, c0:c0 + gd].astype(F32) * sv).astype(o_ref.dtype)


def _sgu_core(zz, norm_w, w_s, b_s_t):
    t = zz.shape[0]
    rows = SGU_CHUNK * SGU_CHUNKS_PER_STEP
    return pl.pallas_call(
        _sgu_core_kernel,
        grid=(t // rows,),
        in_specs=[
            pl.BlockSpec((rows, SGU_WIDTH), lambda c: (c, 0)),
            pl.BlockSpec((rows, SGU_WIDTH), lambda c: (c, 1)),
            pl.BlockSpec((1, SGU_WIDTH), lambda c: (0, 0)),
            pl.BlockSpec((SGU_N_GROUPS, SGU_CHUNK, SGU_CHUNK), lambda c: (0, 0, 0)),
            pl.BlockSpec((SGU_CHUNK, SGU_N_GROUPS), lambda c: (0, 0)),
        ],
        out_specs=pl.BlockSpec((rows, SGU_WIDTH), lambda c: (c, 0)),
        out_shape=jax.ShapeDtypeStruct((t, SGU_WIDTH), BF16),
        compiler_params=_params("parallel"),
        name="sgu_core",
    )(zz, zz, norm_w, w_s, b_s_t)


def _sgu_mixer(h, mix_norm3, layer, j, sgu_w_in, sgu_b_in, sgu_norm, sgu_w_s, sgu_b_s, sgu_w_out):
    zz = _sgu_in(h, mix_norm3, sgu_w_in, sgu_b_in.reshape(sgu_b_in.shape[0], 1, -1), layer, j)
    y = _sgu_core(zz, sgu_norm[j].reshape(1, -1), sgu_w_s[j], sgu_b_s[j].T)
    return _mm_res(y, sgu_w_out, j, h)


def kernel(x, ffn_norm, ffn_w_in, ffn_w_out, mix_norm, ssd_w_in, ssd_conv_w, ssd_conv_b, ssd_dt_bias,
           ssd_a_log, ssd_d, ssd_norm, ssd_w_out, gla_w_in, gla_w_gate2, gla_b_gate, gla_norm, gla_w_out,
           sgu_w_in, sgu_b_in, sgu_norm, sgu_w_s, sgu_b_s, sgu_w_out, final_norm):
    b, t, d = x.shape
    assert (b, t, d) == (1, SEQ, D_MODEL)
    h = x.reshape(t, d)
    ffn_norm4 = ffn_norm.reshape(DEPTH, 2, 1, D_MODEL)
    mix_norm3 = mix_norm.reshape(DEPTH, 1, D_MODEL)
    final_w = final_norm.reshape(1, D_MODEL)
    for i in range(DEPTH):
        h = _ffn(h, ffn_norm4, ffn_w_in, ffn_w_out, final_w, i, 0, False)
        kind, j = i % N_MIXERS, i // N_MIXERS
        if kind == 0:
            h = _ssd_mixer(h, mix_norm3, i, j, ssd_w_in, ssd_conv_w, ssd_conv_b, ssd_dt_bias, ssd_a_log,
                           ssd_d, ssd_norm, ssd_w_out)
        elif kind == 1:
            h = _gla_mixer(h, mix_norm3, i, j, gla_w_in, gla_w_gate2, gla_b_gate, gla_norm, gla_w_out)
        else:
            h = _sgu_mixer(h, mix_norm3, i, j, sgu_w_in, sgu_b_in, sgu_norm, sgu_w_s, sgu_b_s, sgu_w_out)
        h = _ffn(h, ffn_norm4, ffn_w_in, ffn_w_out, final_w, i, 1, i == DEPTH - 1)
    return h.reshape(b, t, d)
```

```python
import functools

import jax
import jax.numpy as jnp
import numpy as np
from jax import lax
from jax.experimental import pallas as pl
from jax.experimental.pallas import tpu as pltpu

F32 = jnp.float32
BF16 = jnp.bfloat16

D_MODEL = 2048
SEQ = 8192
DEPTH = 4
N_MIXERS = 3
NORM_EPS = 1e-6
D_FF = 5632

SSD_D_INNER = 4096
SSD_HEAD_DIM = 64
SSD_N_HEADS = 64
SSD_N_GROUPS = 8
SSD_HEADS_PER_GROUP = 8
SSD_D_STATE = 128
SSD_CONV = 4
SSD_CHUNK = 128
SSD_CONV_DIM = 6144
SSD_GROUP_WIDTH = SSD_HEADS_PER_GROUP * SSD_HEAD_DIM
SSD_MAIN_DIM = SSD_D_INNER + SSD_CONV_DIM

GLA_N_HEADS = 4
GLA_D_K = 1024
GLA_D_V = 2048
GLA_HEAD_K = 256
GLA_HEAD_V = 512
GLA_GATE_RANK = 16
GLA_GATE_TAU = 16.0
GLA_CHUNK = 64
GLA_MAIN_DIM = 2 * GLA_D_K + 2 * GLA_D_V

SGU_WIDTH = 4096
SGU_N_GROUPS = 8
SGU_GROUP_DIM = 512
SGU_CHUNK = 128

FFN_TM = 1024
FFN_TF = 512
PROJ_TM = 2048
PROJ_TN = 512
OUT_TM = 512
OUT_TN = 1024
SSD_CHUNKS_PER_STEP = 2
GLA_CHUNKS_PER_STEP = 4
SGU_CHUNKS_PER_STEP = 4
LANES = 128
BF16_ROWS = 16
CONV_TAIL = BF16_ROWS
CONV_K = -(-(SSD_CONV - 1) * (SSD_CHUNK + CONV_TAIL) // LANES) * LANES
LOG2E = 1.4426950408889634

VMEM_LIMIT = 56 * 1024 * 1024
BIG_VMEM_LIMIT = 62 * 1024 * 1024


def _silu(x):
    hx = 0.5 * x
    return hx * jnp.tanh(hx) + hx


def _softplus(x):
    return jnp.maximum(x, 0.0) + jnp.log1p(jnp.exp(-jnp.abs(x)))


def _rms(x, w):
    return x * lax.rsqrt(jnp.mean(x * x, axis=-1, keepdims=True) + NORM_EPS) * w


def _dot(a, b):
    return jnp.dot(a, b, preferred_element_type=F32)


def _dot_nt(a, b):
    return lax.dot_general(a, b, (((1,), (1,)), ((), ())), preferred_element_type=F32)


def _dot_tn(a, b):
    return lax.dot_general(a, b, (((0,), (0,)), ((), ())), preferred_element_type=F32)


def _split3(x):
    x1 = x.astype(BF16)
    r1 = x - x1.astype(F32)
    x2 = r1.astype(BF16)
    r2 = r1 - x2.astype(F32)
    return x1, x2, r2.astype(BF16)


def _params(*sem):
    return pltpu.CompilerParams(dimension_semantics=sem, vmem_limit_bytes=VMEM_LIMIT)


def _ffn_kernel(h_hbm, g_ref, wg_ref, wu_ref, wo_ref, fw_ref, o_hbm, n_ref, acc_ref, in_sem, out_sem, *, final):
    i = pl.program_id(0)
    j = pl.program_id(1)
    n_tiles = pl.num_programs(0)
    slot = i % 2
    other = 1 - slot

    def h_copy(tile, s):
        return pltpu.make_async_copy(h_hbm.at[pl.ds(tile * FFN_TM, FFN_TM), :], acc_ref.at[s], in_sem.at[s])

    def o_copy(tile, s):
        return pltpu.make_async_copy(acc_ref.at[s], o_hbm.at[pl.ds(tile * FFN_TM, FFN_TM), :], out_sem.at[s])

    @pl.when(j == 0)
    def _():
        @pl.when(i == 0)
        def _():
            h_copy(0, 0).start()

        h_copy(i, slot).wait()
        n_ref[...] = _rms(acc_ref[slot], g_ref[...]).astype(BF16)

    @pl.when(jnp.logical_and(j == 1, i + 1 < n_tiles))
    def _():
        @pl.when(i >= 1)
        def _():
            o_copy(i - 1, other).wait()

        h_copy(i + 1, other).start()

    n = n_ref[...]
    gate = _dot(n, wg_ref[...].astype(BF16))
    up = _dot(n, wu_ref[...].astype(BF16))
    act = (0.5 * _silu(gate) * up).astype(BF16)
    acc_ref[slot] += _dot(act, wo_ref[...].astype(BF16))

    @pl.when(j == pl.num_programs(1) - 1)
    def _():
        if final:
            acc_ref[slot] = _rms(acc_ref[slot], fw_ref[...])
        o_copy(i, slot).start()

        @pl.when(i == n_tiles - 1)
        def _():
            o_copy(i, slot).wait()

            @pl.when(i >= 1)
            def _():
                o_copy(i - 1, other).wait()


def _ffn(h, ffn_norm4, ffn_w_in, ffn_w_out, final_w, layer, half, final):
    t = h.shape[0]
    nf = D_FF // FFN_TF
    return pl.pallas_call(
        functools.partial(_ffn_kernel, final=final),
        grid=(t // FFN_TM, nf),
        in_specs=[
            pl.BlockSpec(memory_space=pl.ANY),
            pl.BlockSpec((None, None, 1, D_MODEL), lambda i, j: (layer, half, 0, 0)),
            pl.BlockSpec((None, None, D_MODEL, FFN_TF), lambda i, j: (layer, half, 0, j)),
            pl.BlockSpec((None, None, D_MODEL, FFN_TF), lambda i, j: (layer, half, 0, j + nf)),
            pl.BlockSpec((None, None, FFN_TF, D_MODEL), lambda i, j: (layer, half, j, 0)),
            pl.BlockSpec((1, D_MODEL), lambda i, j: (0, 0)),
        ],
        out_specs=pl.BlockSpec(memory_space=pl.ANY),
        out_shape=jax.ShapeDtypeStruct((t, D_MODEL), F32),
        scratch_shapes=[
            pltpu.VMEM((FFN_TM, D_MODEL), BF16),
            pltpu.VMEM((2, FFN_TM, D_MODEL), F32),
            pltpu.SemaphoreType.DMA((2,)),
            pltpu.SemaphoreType.DMA((2,)),
        ],
        compiler_params=pltpu.CompilerParams(dimension_semantics=("arbitrary", "arbitrary"),
                                             vmem_limit_bytes=BIG_VMEM_LIMIT),
        name="ffn",
    )(h, ffn_norm4, ffn_w_in, ffn_w_in, ffn_w_out, final_w)


def _mm_res_kernel(a_ref, w_ref, r_ref, o_ref, wb_ref):
    @pl.when(pl.program_id(1) == 0)
    def _():
        wb_ref[...] = w_ref[...].astype(BF16)

    o_ref[...] = r_ref[...] + _dot(a_ref[...], wb_ref[...])


def _mm_res(a, w, layer, res):
    t, k = a.shape
    return pl.pallas_call(
        _mm_res_kernel,
        grid=(D_MODEL // OUT_TN, t // OUT_TM),
        in_specs=[
            pl.BlockSpec((OUT_TM, k), lambda j, i: (i, 0)),
            pl.BlockSpec((None, k, OUT_TN), lambda j, i: (layer, 0, j)),
            pl.BlockSpec((OUT_TM, OUT_TN), lambda j, i: (i, j)),
        ],
        out_specs=pl.BlockSpec((OUT_TM, OUT_TN), lambda j, i: (i, j)),
        out_shape=jax.ShapeDtypeStruct((t, D_MODEL), F32),
        scratch_shapes=[pltpu.VMEM((k, OUT_TN), BF16)],
        compiler_params=pltpu.CompilerParams(dimension_semantics=("parallel", "arbitrary"),
                                             vmem_limit_bytes=BIG_VMEM_LIMIT),
        name="out_proj",
    )(a, w, res)


def _narrow_rows(w_ref, valid):
    row = lax.broadcasted_iota(jnp.int32, (LANES, 1), 0)
    return jnp.where(row < valid, w_ref[...], 0.0).astype(BF16)


def _normed_rows(h_hbm, g_ref, n_ref, hbuf_ref, sem):
    i = pl.program_id(0)

    def h_copy(tile):
        return pltpu.make_async_copy(h_hbm.at[pl.ds(tile * PROJ_TM, PROJ_TM), :], hbuf_ref, sem.at[0])

    @pl.when(i == 0)
    def _():
        h_copy(0).start()

    h_copy(i).wait()
    n = _rms(hbuf_ref[...], g_ref[...]).astype(BF16)
    n_ref[...] = n

    @pl.when(i + 1 < pl.num_programs(0))
    def _():
        h_copy(i + 1).start()

    return n


PROJ_SCRATCH = [
    pltpu.VMEM((PROJ_TM, D_MODEL), BF16),
    pltpu.VMEM((PROJ_TM, D_MODEL), F32),
    pltpu.SemaphoreType.DMA((1,)),
]


def _ssd_in_kernel(h_hbm, g_ref, w_ref, wdt_ref, o_ref, dt_ref, n_ref, hbuf_ref, sem):
    @pl.when(pl.program_id(1) == 0)
    def _():
        n = _normed_rows(h_hbm, g_ref, n_ref, hbuf_ref, sem)
        dt_ref[...] = _dot_nt(n, _narrow_rows(wdt_ref, SSD_N_HEADS))

    o_ref[...] = _dot_nt(n_ref[...], w_ref[...].astype(BF16)).astype(o_ref.dtype)


def _ssd_in(h, mix_norm3, ssd_w_in_t, layer, j):
    t = h.shape[0]
    return pl.pallas_call(
        _ssd_in_kernel,
        grid=(t // PROJ_TM, SSD_MAIN_DIM // PROJ_TN),
        in_specs=[
            pl.BlockSpec(memory_space=pl.ANY),
            pl.BlockSpec((None, 1, D_MODEL), lambda i, c: (layer, 0, 0)),
            pl.BlockSpec((None, PROJ_TN, D_MODEL), lambda i, c: (j, c, 0)),
            pl.BlockSpec((None, LANES, D_MODEL), lambda i, c: (j, SSD_MAIN_DIM // LANES, 0)),
        ],
        out_specs=[
            pl.BlockSpec((PROJ_TM, PROJ_TN), lambda i, c: (i, c)),
            pl.BlockSpec((PROJ_TM, LANES), lambda i, c: (i, 0)),
        ],
        out_shape=[
            jax.ShapeDtypeStruct((t, SSD_MAIN_DIM), BF16),
            jax.ShapeDtypeStruct((t, LANES), F32),
        ],
        scratch_shapes=PROJ_SCRATCH,
        compiler_params=_params("arbitrary", "arbitrary"),
        name="ssd_in",
    )(h, mix_norm3, ssd_w_in_t, ssd_w_in_t)


def _conv_shift_matrix():
    q, blk = SSD_CHUNK, SSD_CHUNK + CONV_TAIL
    s = np.zeros((q, CONV_K), np.float32)
    for k in range(SSD_CONV - 1):
        shift = SSD_CONV - 1 - k
        for t in range(q):
            src = t - shift
            s[t, k * blk + (src if src >= 0 else q + CONV_TAIL + src)] = 1.0
    return jnp.asarray(s, BF16)


def _ssd_core_kernel(z_ref, xs_ref, bc_ref, dt_ref, cw_ref, cb_ref, dtb_ref, al_ref, dex_ref, nw_ref,
                     e_ref, e3_ref, sh_ref, o_ref, st_ref, tail_ref):
    q = SSD_CHUNK
    gw = SSD_GROUP_WIDTH
    ns = SSD_D_STATE

    @pl.when(pl.program_id(0) == 0)
    def _():
        st_ref[...] = jnp.zeros_like(st_ref)
        tail_ref[...] = jnp.zeros_like(tail_ref)

    row = lax.broadcasted_iota(jnp.int32, (q, q), 0)
    col = lax.broadcasted_iota(jnp.int32, (q, q), 1)
    tril = row >= col
    lower = jnp.where(tril, 1.0, 0.0).astype(BF16)
    lane = lax.broadcasted_iota(jnp.int32, (q, 2 * SSD_HEAD_DIM), 1)
    even_head = lane < SSD_HEAD_DIM

    for ci in range(SSD_CHUNKS_PER_STEP):
        r0 = ci * q

        def conv(src_ref, src_lo, lo, width):
            cur = src_ref[r0:r0 + q, src_lo:src_lo + width]
            if ci == 0:
                tail = tail_ref[:, lo:lo + width]
            else:
                tail = src_ref[r0 - CONV_TAIL:r0, src_lo:src_lo + width]
            parts = []
            for k in range(SSD_CONV - 1):
                wk = cw_ref[k:k + 1, lo:lo + width].astype(BF16)
                parts += [cur * wk, tail * wk]
            parts.append(jnp.zeros((CONV_K - (SSD_CONV - 1) * (q + CONV_TAIL), width), BF16))
            acc = _dot(sh_ref[...], jnp.concatenate(parts, axis=0))
            acc = acc + cur.astype(F32) * cw_ref[SSD_CONV - 1:SSD_CONV, lo:lo + width] + cb_ref[:, lo:lo + width]
            return _silu(acc)

        dt = _softplus(dt_ref[r0:r0 + q, :] + dtb_ref[...])
        da = dt * (-LOG2E * jnp.exp(al_ref[...]))
        acum = sum(_dot(lower, p) for p in _split3(da))
        acum_t = acum.T
        acum3 = jnp.concatenate(_split3(acum), axis=1)
        dt16 = dt.astype(BF16)

        for g in range(SSD_N_GROUPS):
            c0 = g * gw
            ae = _dot(acum3, e3_ref[:, c0:c0 + gw])
            dte = _dot(dt16, e_ref[:, c0:c0 + gw])
            xs = conv(xs_ref, c0, c0, gw)
            b0 = g * ns
            c1 = SSD_N_GROUPS * ns + g * ns
            bg = conv(bc_ref, b0, SSD_D_INNER + b0, ns).astype(BF16)
            cg = conv(bc_ref, c1, SSD_D_INNER + c1, ns).astype(BF16)
            cbm = _dot_nt(cg, bg)
            xdt = xs * dte

            st = st_ref[g]
            y = _dot(cg, st.astype(BF16)) * jnp.exp2(ae)

            pieces = []
            for pr in range(SSD_HEADS_PER_GROUP // 2):
                def scores(hh):
                    seg = acum[:, hh:hh + 1] - acum_t[hh:hh + 1, :]
                    return (cbm * jnp.exp2(jnp.where(tril, seg, -jnp.inf))).astype(BF16)

                h0 = g * SSD_HEADS_PER_GROUP + 2 * pr
                lhs = jnp.concatenate([scores(h0), scores(h0 + 1)], axis=1)
                l0 = pr * 2 * SSD_HEAD_DIM
                xp = xdt[:, l0:l0 + 2 * SSD_HEAD_DIM]
                rhs = jnp.concatenate([jnp.where(even_head, xp, 0.0),
                                       jnp.where(even_head, 0.0, xp)], axis=0)
                pieces.append(_dot(lhs, rhs.astype(BF16)))
            y = y + jnp.concatenate(pieces, axis=1)

            last = ae[q - 1:q, :]
            to_end = jnp.exp2(last - ae)
            st_ref[g] = st * jnp.exp2(last) + _dot_tn(bg, (xdt * to_end).astype(BF16))

            y = y + dex_ref[:, c0:c0 + gw] * xs
            gated = y * _silu(z_ref[r0:r0 + q, c0:c0 + gw].astype(F32))
            o_ref[r0:r0 + q, c0:c0 + gw] = _rms(gated, nw_ref[:, c0:c0 + gw]).astype(o_ref.dtype)

    rows = SSD_CHUNKS_PER_STEP * q
    tail_ref[:, 0:SSD_D_INNER] = xs_ref[rows - CONV_TAIL:rows, :]
    tail_ref[:, SSD_D_INNER:SSD_CONV_DIM] = bc_ref[rows - CONV_TAIL:rows, :]


def _ssd_core(zx, dt, conv_w_t, conv_b, dt_bias, a_log, d_exp, norm_w, expand):
    t = zx.shape[0]
    q = SSD_CHUNK
    rows = SSD_CHUNKS_PER_STEP * q
    full = lambda shape: pl.BlockSpec(shape, lambda c: (0,) * len(shape))
    pad_heads = lambda v: jnp.pad(v, (0, LANES - SSD_N_HEADS)).reshape(1, LANES)
    return pl.pallas_call(
        _ssd_core_kernel,
        grid=(t // rows,),
        in_specs=[
            pl.BlockSpec((rows, SSD_D_INNER), lambda c: (c, 0)),
            pl.BlockSpec((rows, SSD_D_INNER), lambda c: (c, 1)),
            pl.BlockSpec((rows, 2 * SSD_N_GROUPS * SSD_D_STATE), lambda c: (c, 4)),
            pl.BlockSpec((rows, LANES), lambda c: (c, 0)),
            full((SSD_CONV, SSD_CONV_DIM)),
            full((1, SSD_CONV_DIM)),
            full((1, LANES)),
            full((1, LANES)),
            full((1, SSD_D_INNER)),
            full((1, SSD_D_INNER)),
            full((LANES, SSD_D_INNER)),
            full((3 * LANES, SSD_D_INNER)),
            full((q, CONV_K)),
        ],
        out_specs=pl.BlockSpec((rows, SSD_D_INNER), lambda c: (c, 0)),
        out_shape=jax.ShapeDtypeStruct((t, SSD_D_INNER), BF16),
        scratch_shapes=[
            pltpu.VMEM((SSD_N_GROUPS, SSD_D_STATE, SSD_GROUP_WIDTH), F32),
            pltpu.VMEM((CONV_TAIL, SSD_CONV_DIM), BF16),
        ],
        compiler_params=_params("arbitrary"),
        name="ssd_core",
    )(zx, zx, zx, dt, conv_w_t, conv_b, pad_heads(dt_bias), pad_heads(a_log), d_exp, norm_w,
      expand, jnp.tile(expand, (3, 1)), _conv_shift_matrix())


def _ssd_mixer(h, mix_norm3, layer, j, ssd_w_in, ssd_conv_w, ssd_conv_b, ssd_dt_bias, ssd_a_log,
               ssd_d, ssd_norm, ssd_w_out):
    zx, dt = _ssd_in(h, mix_norm3, jnp.swapaxes(ssd_w_in, 1, 2), layer, j)
    head_of_channel = jnp.arange(SSD_D_INNER) // SSD_HEAD_DIM
    expand = (jnp.arange(LANES)[:, None] == head_of_channel[None, :]).astype(BF16)
    d_exp = jnp.repeat(ssd_d[j], SSD_HEAD_DIM).reshape(1, SSD_D_INNER)
    yn = _ssd_core(zx, dt, ssd_conv_w[j].T, ssd_conv_b[j].reshape(1, -1), ssd_dt_bias[j],
                   ssd_a_log[j], d_exp, ssd_norm[j].reshape(1, -1), expand)
    return _mm_res(yn, ssd_w_out, j, h)


def _gla_in_kernel(h_hbm, g_ref, w_ref, wl_ref, o_ref, gl_ref, n_ref, hbuf_ref, sem):
    @pl.when(pl.program_id(1) == 0)
    def _():
        n = _normed_rows(h_hbm, g_ref, n_ref, hbuf_ref, sem)
        gl_ref[...] = _dot_nt(n, _narrow_rows(wl_ref, GLA_GATE_RANK))

    o_ref[...] = _dot_nt(n_ref[...], w_ref[...].astype(BF16)).astype(o_ref.dtype)


def _gla_in(h, mix_norm3, gla_w_in_t, layer, j):
    t = h.shape[0]
    return pl.pallas_call(
        _gla_in_kernel,
        grid=(t // PROJ_TM, GLA_MAIN_DIM // PROJ_TN),
        in_specs=[
            pl.BlockSpec(memory_space=pl.ANY),
            pl.BlockSpec((None, 1, D_MODEL), lambda i, c: (layer, 0, 0)),
            pl.BlockSpec((None, PROJ_TN, D_MODEL), lambda i, c: (j, c, 0)),
            pl.BlockSpec((None, LANES, D_MODEL), lambda i, c: (j, GLA_MAIN_DIM // LANES, 0)),
        ],
        out_specs=[
            pl.BlockSpec((PROJ_TM, PROJ_TN), lambda i, c: (i, c)),
            pl.BlockSpec((PROJ_TM, LANES), lambda i, c: (i, 0)),
        ],
        out_shape=[
            jax.ShapeDtypeStruct((t, GLA_MAIN_DIM), BF16),
            jax.ShapeDtypeStruct((t, LANES), F32),
        ],
        scratch_shapes=PROJ_SCRATCH,
        compiler_params=_params("arbitrary", "arbitrary"),
        name="gla_in",
    )(h, mix_norm3, gla_w_in_t, gla_w_in_t)


def _gla_core_kernel(x_ref, gl_ref, w2_ref, bg_ref, nw_ref, o_ref, st_ref):
    q = GLA_CHUNK
    hk, hv = GLA_HEAD_K, GLA_HEAD_V

    @pl.when(pl.program_id(0) == 0)
    def _():
        st_ref[...] = jnp.zeros_like(st_ref)

    row = lax.broadcasted_iota(jnp.int32, (q, q), 0)
    col = lax.broadcasted_iota(jnp.int32, (q, q), 1)
    tril = row >= col
    lower = jnp.where(tril, 1.0, 0.0).astype(BF16)
    w2 = w2_ref[...].astype(BF16)

    def cum_log_decay(r0):
        lg = _dot(gl_ref[r0:r0 + q, :].astype(BF16), w2) + bg_ref[...]
        log_a = (jnp.minimum(lg, 0.0) - jnp.log1p(jnp.exp(-jnp.abs(lg)))) * (LOG2E / GLA_GATE_TAU)
        return sum(_dot(lower, p) for p in _split3(log_a))

    for pair in range(GLA_CHUNKS_PER_STEP // 2):
        ra = 2 * pair * q
        rb = ra + q
        bcum_a = cum_log_decay(ra)
        bcum_b = cum_log_decay(rb)
        for hd in range(GLA_N_HEADS):
            k0 = hd * hk
            v0 = 2 * GLA_D_K + hd * hv

            def chunk(r0, bcum):
                qh = x_ref[r0:r0 + q, k0:k0 + hk].astype(F32) * (hk ** -0.5)
                kh = x_ref[r0:r0 + q, GLA_D_K + k0:GLA_D_K + k0 + hk].astype(F32)
                vh = x_ref[r0:r0 + q, v0:v0 + hv]
                bc = bcum[:, k0:k0 + hk]
                mid = bc[q // 2:q // 2 + 1, :]
                end = bc[q - 1:q, :]
                att = _dot_nt((qh * jnp.exp2(bc - mid)).astype(BF16), (kh * jnp.exp2(mid - bc)).astype(BF16))
                att = jnp.where(tril, att, 0.0)
                o_intra = _dot(att.astype(BF16), vh)
                return o_intra, qh * jnp.exp2(bc), kh * jnp.exp2(end - bc), vh, jnp.exp2(end)

            oa, qin_a, kout_a, va, dec_a = chunk(ra, bcum_a)
            ob, qin_b, kout_b, vb, dec_b = chunk(rb, bcum_b)
            st = st_ref[hd]
            q2 = jnp.concatenate([qin_a, qin_b * dec_a], axis=0).astype(BF16)
            inter = _dot_nt(q2, st.astype(BF16))
            cross = _dot_nt(qin_b.astype(BF16), kout_a.astype(BF16))
            oa = oa + inter[0:q]
            ob = ob + inter[q:2 * q] + _dot(cross.astype(BF16), va)
            k2 = jnp.concatenate([kout_a * dec_b, kout_b], axis=0).astype(BF16)
            v2 = jnp.concatenate([va, vb], axis=0)
            st_ref[hd] = st * (dec_a * dec_b) + _dot_tn(v2, k2)
            for r0, o in ((ra, oa), (rb, ob)):
                rh = x_ref[r0:r0 + q, v0 + GLA_D_V:v0 + GLA_D_V + hv].astype(F32)
                o = _rms(o, nw_ref[...]) * _silu(rh)
                o_ref[r0:r0 + q, hd * hv:(hd + 1) * hv] = o.astype(o_ref.dtype)


def _gla_core(qkvr, g_low, w_gate2, b_gate, norm_w):
    t = qkvr.shape[0]
    rows = GLA_CHUNK * GLA_CHUNKS_PER_STEP
    return pl.pallas_call(
        _gla_core_kernel,
        grid=(t // rows,),
        in_specs=[
            pl.BlockSpec((rows, GLA_MAIN_DIM), lambda c: (c, 0)),
            pl.BlockSpec((rows, LANES), lambda c: (c, 0)),
            pl.BlockSpec((LANES, GLA_D_K), lambda c: (0, 0)),
            pl.BlockSpec((1, GLA_D_K), lambda c: (0, 0)),
            pl.BlockSpec((1, GLA_HEAD_V), lambda c: (0, 0)),
        ],
        out_specs=pl.BlockSpec((rows, GLA_D_V), lambda c: (c, 0)),
        out_shape=jax.ShapeDtypeStruct((t, GLA_D_V), BF16),
        scratch_shapes=[pltpu.VMEM((GLA_N_HEADS, GLA_HEAD_V, GLA_HEAD_K), F32)],
        compiler_params=_params("arbitrary"),
        name="gla_core",
    )(qkvr, g_low, w_gate2, b_gate, norm_w)


def _gla_mixer(h, mix_norm3, layer, j, gla_w_in, gla_w_gate2, gla_b_gate, gla_norm, gla_w_out):
    qkvr, g_low = _gla_in(h, mix_norm3, jnp.swapaxes(gla_w_in, 1, 2), layer, j)
    w_gate2 = jnp.pad(gla_w_gate2[j], ((0, LANES - GLA_GATE_RANK), (0, 0)))
    o = _gla_core(qkvr, g_low, w_gate2, gla_b_gate[j].reshape(1, -1), gla_norm[j].reshape(1, -1))
    return _mm_res(o, gla_w_out, j, h)


def _sgu_in_kernel(h_hbm, g_ref, w_ref, b_ref, o_ref, n_ref, hbuf_ref, sem):
    @pl.when(pl.program_id(1) == 0)
    def _():
        _normed_rows(h_hbm, g_ref, n_ref, hbuf_ref, sem)

    y = _dot(n_ref[...], w_ref[...].astype(BF16)) + b_ref[...]
    o_ref[...] = (0.5 * y * (1.0 + lax.erf(y * (0.5 ** 0.5)))).astype(o_ref.dtype)


def _sgu_in(h, mix_norm3, sgu_w_in, sgu_b_in3, layer, j):
    t = h.shape[0]
    return pl.pallas_call(
        _sgu_in_kernel,
        grid=(t // PROJ_TM, 2 * SGU_WIDTH // PROJ_TN),
        in_specs=[
            pl.BlockSpec(memory_space=pl.ANY),
            pl.BlockSpec((None, 1, D_MODEL), lambda i, c: (layer, 0, 0)),
            pl.BlockSpec((None, D_MODEL, PROJ_TN), lambda i, c: (j, 0, c)),
            pl.BlockSpec((None, 1, PROJ_TN), lambda i, c: (j, 0, c)),
        ],
        out_specs=pl.BlockSpec((PROJ_TM, PROJ_TN), lambda i, c: (i, c)),
        out_shape=jax.ShapeDtypeStruct((t, 2 * SGU_WIDTH), BF16),
        scratch_shapes=PROJ_SCRATCH,
        compiler_params=_params("arbitrary", "arbitrary"),
        name="sgu_in",
    )(h, mix_norm3, sgu_w_in, sgu_b_in3)


def _sgu_core_kernel(u_ref, v_ref, nw_ref, ws_ref, bs_ref, o_ref):
    q = SGU_CHUNK
    gd = SGU_GROUP_DIM
    row = lax.broadcasted_iota(jnp.int32, (q, q), 0)
    col = lax.broadcasted_iota(jnp.int32, (q, q), 1)
    tril = row >= col
    for ci in range(SGU_CHUNKS_PER_STEP):
        r0 = ci * q
        vn = _rms(v_ref[r0:r0 + q, :].astype(F32), nw_ref[...]).astype(BF16)
        for g in range(SGU_N_GROUPS):
            c0 = g * gd
            wc = jnp.where(tril, ws_ref[g], 0.0).astype(BF16)
            sv = _dot(wc, vn[:, c0:c0 + gd]) + bs_ref[:, g:g + 1]
            o_ref[r0:r0 + q, c0:c0 + gd] = (u_ref[r0:r0 + q, c0:c0 + gd].astype(F32) * sv).astype(o_ref.dtype)


def _sgu_core(zz, norm_w, w_s, b_s_t):
    t = zz.shape[0]
    rows = SGU_CHUNK * SGU_CHUNKS_PER_STEP
    return pl.pallas_call(
        _sgu_core_kernel,
        grid=(t // rows,),
        in_specs=[
            pl.BlockSpec((rows, SGU_WIDTH), lambda c: (c, 0)),
            pl.BlockSpec((rows, SGU_WIDTH), lambda c: (c, 1)),
            pl.BlockSpec((1, SGU_WIDTH), lambda c: (0, 0)),
            pl.BlockSpec((SGU_N_GROUPS, SGU_CHUNK, SGU_CHUNK), lambda c: (0, 0, 0)),
            pl.BlockSpec((SGU_CHUNK, SGU_N_GROUPS), lambda c: (0, 0)),
        ],
        out_specs=pl.BlockSpec((rows, SGU_WIDTH), lambda c: (c, 0)),
        out_shape=jax.ShapeDtypeStruct((t, SGU_WIDTH), BF16),
        compiler_params=_params("parallel"),
        name="sgu_core",
    )(zz, zz, norm_w, w_s, b_s_t)


def _sgu_mixer(h, mix_norm3, layer, j, sgu_w_in, sgu_b_in, sgu_norm, sgu_w_s, sgu_b_s, sgu_w_out):
    zz = _sgu_in(h, mix_norm3, sgu_w_in, sgu_b_in.reshape(sgu_b_in.shape[0], 1, -1), layer, j)
    y = _sgu_core(zz, sgu_norm[j].reshape(1, -1), sgu_w_s[j], sgu_b_s[j].T)
    return _mm_res(y, sgu_w_out, j, h)


def kernel(x, ffn_norm, ffn_w_in, ffn_w_out, mix_norm, ssd_w_in, ssd_conv_w, ssd_conv_b, ssd_dt_bias,
           ssd_a_log, ssd_d, ssd_norm, ssd_w_out, gla_w_in, gla_w_gate2, gla_b_gate, gla_norm, gla_w_out,
           sgu_w_in, sgu_b_in, sgu_norm, sgu_w_s, sgu_b_s, sgu_w_out, final_norm):
    b, t, d = x.shape
    assert (b, t, d) == (1, SEQ, D_MODEL)
    h = x.reshape(t, d)
    ffn_norm4 = ffn_norm.reshape(DEPTH, 2, 1, D_MODEL)
    mix_norm3 = mix_norm.reshape(DEPTH, 1, D_MODEL)
    final_w = final_norm.reshape(1, D_MODEL)
    for i in range(DEPTH):
        h = _ffn(h, ffn_norm4, ffn_w_in, ffn_w_out, final_w, i, 0, False)
        kind, j = i % N_MIXERS, i // N_MIXERS
        if kind == 0:
            h = _ssd_mixer(h, mix_norm3, i, j, ssd_w_in, ssd_conv_w, ssd_conv_b, ssd_dt_bias, ssd_a_log,
                           ssd_d, ssd_norm, ssd_w_out)
        elif kind == 1:
            h = _gla_mixer(h, mix_norm3, i, j, gla_w_in, gla_w_gate2, gla_b_gate, gla_norm, gla_w_out)
        else:
            h = _sgu_mixer(h, mix_norm3, i, j, sgu_w_in, sgu_b_in, sgu_norm, sgu_w_s, sgu_b_s, sgu_w_out)
        h = _ffn(h, ffn_norm4, ffn_w_in, ffn_w_out, final_w, i, 1, i == DEPTH - 1)
    return h.reshape(b, t, d)
```

```python
import functools

import jax
import jax.numpy as jnp
import numpy as np
from jax import lax
from jax.experimental import pallas as pl
from jax.experimental.pallas import tpu as pltpu

F32 = jnp.float32
BF16 = jnp.bfloat16

D_MODEL = 2048
SEQ = 8192
DEPTH = 4
N_MIXERS = 3
NORM_EPS = 1e-6
D_FF = 5632

SSD_D_INNER = 4096
SSD_HEAD_DIM = 64
SSD_N_HEADS = 64
SSD_N_GROUPS = 8
SSD_HEADS_PER_GROUP = 8
SSD_D_STATE = 128
SSD_CONV = 4
SSD_CHUNK = 128
SSD_CONV_DIM = 6144
SSD_GROUP_WIDTH = SSD_HEADS_PER_GROUP * SSD_HEAD_DIM
SSD_MAIN_DIM = SSD_D_INNER + SSD_CONV_DIM

GLA_N_HEADS = 4
GLA_D_K = 1024
GLA_D_V = 2048
GLA_HEAD_K = 256
GLA_HEAD_V = 512
GLA_GATE_RANK = 16
GLA_GATE_TAU = 16.0
GLA_CHUNK = 64
GLA_MAIN_DIM = 2 * GLA_D_K + 2 * GLA_D_V

SGU_WIDTH = 4096
SGU_N_GROUPS = 8
SGU_GROUP_DIM = 512
SGU_CHUNK = 128

FFN_TM = 1024
FFN_TF = 512
PROJ_TM = 2048
PROJ_TN = 512
OUT_TM = 512
OUT_TN = 1024
SSD_CHUNKS_PER_STEP = 2
GLA_CHUNKS_PER_STEP = 4
SGU_CHUNKS_PER_STEP = 4
LANES = 128
BF16_ROWS = 16
CONV_TAIL = BF16_ROWS
CONV_K = -(-(SSD_CONV - 1) * (SSD_CHUNK + CONV_TAIL) // LANES) * LANES
LOG2E = 1.4426950408889634

VMEM_LIMIT = 56 * 1024 * 1024
BIG_VMEM_LIMIT = 62 * 1024 * 1024


def _silu(x):
    hx = 0.5 * x
    return hx * jnp.tanh(hx) + hx


def _softplus(x):
    return jnp.maximum(x, 0.0) + jnp.log1p(jnp.exp(-jnp.abs(x)))


def _rms(x, w):
    return x * lax.rsqrt(jnp.mean(x * x, axis=-1, keepdims=True) + NORM_EPS) * w


def _dot(a, b):
    return jnp.dot(a, b, preferred_element_type=F32)


def _dot_nt(a, b):
    return lax.dot_general(a, b, (((1,), (1,)), ((), ())), preferred_element_type=F32)


def _dot_tn(a, b):
    return lax.dot_general(a, b, (((0,), (0,)), ((), ())), preferred_element_type=F32)


def _split3(x):
    x1 = x.astype(BF16)
    r1 = x - x1.astype(F32)
    x2 = r1.astype(BF16)
    r2 = r1 - x2.astype(F32)
    return x1, x2, r2.astype(BF16)


def _params(*sem):
    return pltpu.CompilerParams(dimension_semantics=sem, vmem_limit_bytes=VMEM_LIMIT)


def _ffn_kernel(h_hbm, g_ref, wg_ref, wu_ref, wo_ref, fw_ref, o_hbm, n_ref, acc_ref, in_sem, out_sem, *, final):
    i = pl.program_id(0)
    j = pl.program_id(1)
    n_tiles = pl.num_programs(0)
    slot = i % 2
    other = 1 - slot

    def h_copy(tile, s):
        return pltpu.make_async_copy(h_hbm.at[pl.ds(tile * FFN_TM, FFN_TM), :], acc_ref.at[s], in_sem.at[s])

    def o_copy(tile, s):
        return pltpu.make_async_copy(acc_ref.at[s], o_hbm.at[pl.ds(tile * FFN_TM, FFN_TM), :], out_sem.at[s])

    @pl.when(j == 0)
    def _():
        @pl.when(i == 0)
        def _():
            h_copy(0, 0).start()

        h_copy(i, slot).wait()
        n_ref[...] = _rms(acc_ref[slot], g_ref[...]).astype(BF16)

    @pl.when(jnp.logical_and(j == 1, i + 1 < n_tiles))
    def _():
        @pl.when(i >= 1)
        def _():
            o_copy(i - 1, other).wait()

        h_copy(i + 1, other).start()

    n = n_ref[...]
    gate = _dot(n, wg_ref[...].astype(BF16))
    up = _dot(n, wu_ref[...].astype(BF16))
    act = (0.5 * _silu(gate) * up).astype(BF16)
    acc_ref[slot] += _dot(act, wo_ref[...].astype(BF16))

    @pl.when(j == pl.num_programs(1) - 1)
    def _():
        if final:
            acc_ref[slot] = _rms(acc_ref[slot], fw_ref[...])
        o_copy(i, slot).start()

        @pl.when(i == n_tiles - 1)
        def _():
            o_copy(i, slot).wait()

            @pl.when(i >= 1)
            def _():
                o_copy(i - 1, other).wait()


def _ffn(h, ffn_norm4, ffn_w_in, ffn_w_out, final_w, layer, half, final):
    t = h.shape[0]
    nf = D_FF // FFN_TF
    return pl.pallas_call(
        functools.partial(_ffn_kernel, final=final),
        grid=(t // FFN_TM, nf),
        in_specs=[
            pl.BlockSpec(memory_space=pl.ANY),
            pl.BlockSpec((None, None, 1, D_MODEL), lambda i, j: (layer, half, 0, 0)),
            pl.BlockSpec((None, None, D_MODEL, FFN_TF), lambda i, j: (layer, half, 0, j)),
            pl.BlockSpec((None, None, D_MODEL, FFN_TF), lambda i, j: (layer, half, 0, j + nf)),
            pl.BlockSpec((None, None, FFN_TF, D_MODEL), lambda i, j: (layer, half, j, 0)),
            pl.BlockSpec((1, D_MODEL), lambda i, j: (0, 0)),
        ],
        out_specs=pl.BlockSpec(memory_space=pl.ANY),
        out_shape=jax.ShapeDtypeStruct((t, D_MODEL), F32),
        scratch_shapes=[
            pltpu.VMEM((FFN_TM, D_MODEL), BF16),
            pltpu.VMEM((2, FFN_TM, D_MODEL), F32),
            pltpu.SemaphoreType.DMA((2,)),
            pltpu.SemaphoreType.DMA((2,)),
        ],
        compiler_params=pltpu.CompilerParams(dimension_semantics=("arbitrary", "arbitrary"),
                                             vmem_limit_bytes=BIG_VMEM_LIMIT),
        name="ffn",
    )(h, ffn_norm4, ffn_w_in, ffn_w_in, ffn_w_out, final_w)


def _mm_res_kernel(a_ref, w_ref, r_ref, o_ref, wb_ref):
    @pl.when(pl.program_id(1) == 0)
    def _():
        wb_ref[...] = w_ref[...].astype(BF16)

    o_ref[...] = r_ref[...] + _dot(a_ref[...], wb_ref[...])


def _mm_res(a, w, layer, res):
    t, k = a.shape
    return pl.pallas_call(
        _mm_res_kernel,
        grid=(D_MODEL // OUT_TN, t // OUT_TM),
        in_specs=[
            pl.BlockSpec((OUT_TM, k), lambda j, i: (i, 0)),
            pl.BlockSpec((None, k, OUT_TN), lambda j, i: (layer, 0, j)),
            pl.BlockSpec((OUT_TM, OUT_TN), lambda j, i: (i, j)),
        ],
        out_specs=pl.BlockSpec((OUT_TM, OUT_TN), lambda j, i: (i, j)),
        out_shape=jax.ShapeDtypeStruct((t, D_MODEL), F32),
        scratch_shapes=[pltpu.VMEM((k, OUT_TN), BF16)],
        compiler_params=pltpu.CompilerParams(dimension_semantics=("parallel", "arbitrary"),
                                             vmem_limit_bytes=BIG_VMEM_LIMIT),
        name="out_proj",
    )(a, w, res)


def _narrow_rows(w_ref, valid):
    row = lax.broadcasted_iota(jnp.int32, (LANES, 1), 0)
    return jnp.where(row < valid, w_ref[...], 0.0).astype(BF16)


def _normed_rows(h_hbm, g_ref, n_ref, hbuf_ref, sem):
    i = pl.program_id(0)

    def h_copy(tile):
        return pltpu.make_async_copy(h_hbm.at[pl.ds(tile * PROJ_TM, PROJ_TM), :], hbuf_ref, sem.at[0])

    @pl.when(i == 0)
    def _():
        h_copy(0).start()

    h_copy(i).wait()
    n = _rms(hbuf_ref[...], g_ref[...]).astype(BF16)
    n_ref[...] = n

    @pl.when(i + 1 < pl.num_programs(0))
    def _():
        h_copy(i + 1).start()

    return n


PROJ_SCRATCH = [
    pltpu.VMEM((PROJ_TM, D_MODEL), BF16),
    pltpu.VMEM((PROJ_TM, D_MODEL), F32),
    pltpu.SemaphoreType.DMA((1,)),
]


def _ssd_in_kernel(h_hbm, g_ref, w_ref, wdt_ref, o_ref, dt_ref, n_ref, hbuf_ref, sem):
    @pl.when(pl.program_id(1) == 0)
    def _():
        n = _normed_rows(h_hbm, g_ref, n_ref, hbuf_ref, sem)
        dt_ref[...] = _dot_nt(n, _narrow_rows(wdt_ref, SSD_N_HEADS))

    o_ref[...] = _dot_nt(n_ref[...], w_ref[...].astype(BF16)).astype(o_ref.dtype)


def _ssd_in(h, mix_norm3, ssd_w_in_t, layer, j):
    t = h.shape[0]
    return pl.pallas_call(
        _ssd_in_kernel,
        grid=(t // PROJ_TM, SSD_MAIN_DIM // PROJ_TN),
        in_specs=[
            pl.BlockSpec(memory_space=pl.ANY),
            pl.BlockSpec((None, 1, D_MODEL), lambda i, c: (layer, 0, 0)),
            pl.BlockSpec((None, PROJ_TN, D_MODEL), lambda i, c: (j, c, 0)),
            pl.BlockSpec((None, LANES, D_MODEL), lambda i, c: (j, SSD_MAIN_DIM // LANES, 0)),
        ],
        out_specs=[
            pl.BlockSpec((PROJ_TM, PROJ_TN), lambda i, c: (i, c)),
            pl.BlockSpec((PROJ_TM, LANES), lambda i, c: (i, 0)),
        ],
        out_shape=[
            jax.ShapeDtypeStruct((t, SSD_MAIN_DIM), BF16),
            jax.ShapeDtypeStruct((t, LANES), F32),
        ],
        scratch_shapes=PROJ_SCRATCH,
        compiler_params=_params("arbitrary", "arbitrary"),
        name="ssd_in",
    )(h, mix_norm3, ssd_w_in_t, ssd_w_in_t)


def _conv_shift_matrix():
    q, blk = SSD_CHUNK, SSD_CHUNK + CONV_TAIL
    s = np.zeros((q, CONV_K), np.float32)
    for k in range(SSD_CONV - 1):
        shift = SSD_CONV - 1 - k
        for t in range(q):
            src = t - shift
            s[t, k * blk + (src if src >= 0 else q + CONV_TAIL + src)] = 1.0
    return jnp.asarray(s, BF16)


def _ssd_core_kernel(z_ref, xs_ref, bc_ref, dt_ref, cw_ref, cb_ref, dtb_ref, al_ref, dex_ref, nw_ref,
                     e_ref, e3_ref, sh_ref, o_ref, st_ref, tail_ref):
    q = SSD_CHUNK
    gw = SSD_GROUP_WIDTH
    ns = SSD_D_STATE

    @pl.when(pl.program_id(0) == 0)
    def _():
        st_ref[...] = jnp.zeros_like(st_ref)
        tail_ref[...] = jnp.zeros_like(tail_ref)

    row = lax.broadcasted_iota(jnp.int32, (q, q), 0)
    col = lax.broadcasted_iota(jnp.int32, (q, q), 1)
    tril = row >= col
    lower = jnp.where(tril, 1.0, 0.0).astype(BF16)
    lane = lax.broadcasted_iota(jnp.int32, (q, 2 * SSD_HEAD_DIM), 1)
    even_head = lane < SSD_HEAD_DIM

    def conv(ci, src_ref, src_lo, lo, width):
        r0 = ci * q
        cur = src_ref[r0:r0 + q, src_lo:src_lo + width]
        if ci == 0:
            tail = tail_ref[:, lo:lo + width]
        else:
            tail = src_ref[r0 - CONV_TAIL:r0, src_lo:src_lo + width]
        parts = []
        for k in range(SSD_CONV - 1):
            wk = cw_ref[k:k + 1, lo:lo + width].astype(BF16)
            parts += [cur * wk, tail * wk]
        parts.append(jnp.zeros((CONV_K - (SSD_CONV - 1) * (q + CONV_TAIL), width), BF16))
        acc = _dot(sh_ref[...], jnp.concatenate(parts, axis=0))
        acc = acc + cur.astype(F32) * cw_ref[SSD_CONV - 1:SSD_CONV, lo:lo + width] + cb_ref[:, lo:lo + width]
        return _silu(acc)

    decay = []
    for ci in range(SSD_CHUNKS_PER_STEP):
        r0 = ci * q
        dt = _softplus(dt_ref[r0:r0 + q, :] + dtb_ref[...])
        da = dt * (-LOG2E * jnp.exp(al_ref[...]))
        acum = sum(_dot(lower, p) for p in _split3(da))
        decay.append((acum, acum.T, jnp.concatenate(_split3(acum), axis=1), dt.astype(BF16)))

    for g in range(SSD_N_GROUPS):
        st = st_ref[g]
        for ci in range(SSD_CHUNKS_PER_STEP):
            r0 = ci * q
            acum, acum_t, acum3, dt16 = decay[ci]
            c0 = g * gw
            ae = _dot(acum3, e3_ref[:, c0:c0 + gw])
            dte = _dot(dt16, e_ref[:, c0:c0 + gw])
            xs = conv(ci, xs_ref, c0, c0, gw)
            b0 = g * ns
            c1 = SSD_N_GROUPS * ns + g * ns
            bg = conv(ci, bc_ref, b0, SSD_D_INNER + b0, ns).astype(BF16)
            cg = conv(ci, bc_ref, c1, SSD_D_INNER + c1, ns).astype(BF16)
            cbm = _dot_nt(cg, bg)
            xdt = xs * dte

            y = _dot(cg, st.astype(BF16)) * jnp.exp2(ae)

            pieces = []
            for pr in range(SSD_HEADS_PER_GROUP // 2):
                def scores(hh):
                    seg = acum[:, hh:hh + 1] - acum_t[hh:hh + 1, :]
                    return (cbm * jnp.exp2(jnp.where(tril, seg, -jnp.inf))).astype(BF16)

                h0 = g * SSD_HEADS_PER_GROUP + 2 * pr
                lhs = jnp.concatenate([scores(h0), scores(h0 + 1)], axis=1)
                l0 = pr * 2 * SSD_HEAD_DIM
                xp = xdt[:, l0:l0 + 2 * SSD_HEAD_DIM]
                rhs = jnp.concatenate([jnp.where(even_head, xp, 0.0),
                                       jnp.where(even_head, 0.0, xp)], axis=0)
                pieces.append(_dot(lhs, rhs.astype(BF16)))
            y = y + jnp.concatenate(pieces, axis=1)

            last = ae[q - 1:q, :]
            to_end = jnp.exp2(last - ae)
            st_next = st * jnp.exp2(last) + _dot_tn(bg, (xdt * to_end).astype(BF16))

            y = y + dex_ref[:, c0:c0 + gw] * xs
            gated = y * _silu(z_ref[r0:r0 + q, c0:c0 + gw].astype(F32))
            o_ref[r0:r0 + q, c0:c0 + gw] = _rms(gated, nw_ref[:, c0:c0 + gw]).astype(o_ref.dtype)
            st = st_next
        st_ref[g] = st

    rows = SSD_CHUNKS_PER_STEP * q
    tail_ref[:, 0:SSD_D_INNER] = xs_ref[rows - CONV_TAIL:rows, :]
    tail_ref[:, SSD_D_INNER:SSD_CONV_DIM] = bc_ref[rows - CONV_TAIL:rows, :]


def _ssd_core(zx, dt, conv_w_t, conv_b, dt_bias, a_log, d_exp, norm_w, expand):
    t = zx.shape[0]
    q = SSD_CHUNK
    rows = SSD_CHUNKS_PER_STEP * q
    full = lambda shape: pl.BlockSpec(shape, lambda c: (0,) * len(shape))
    pad_heads = lambda v: jnp.pad(v, (0, LANES - SSD_N_HEADS)).reshape(1, LANES)
    return pl.pallas_call(
        _ssd_core_kernel,
        grid=(t // rows,),
        in_specs=[
            pl.BlockSpec((rows, SSD_D_INNER), lambda c: (c, 0)),
            pl.BlockSpec((rows, SSD_D_INNER), lambda c: (c, 1)),
            pl.BlockSpec((rows, 2 * SSD_N_GROUPS * SSD_D_STATE), lambda c: (c, 4)),
            pl.BlockSpec((rows, LANES), lambda c: (c, 0)),
            full((SSD_CONV, SSD_CONV_DIM)),
            full((1, SSD_CONV_DIM)),
            full((1, LANES)),
            full((1, LANES)),
            full((1, SSD_D_INNER)),
            full((1, SSD_D_INNER)),
            full((LANES, SSD_D_INNER)),
            full((3 * LANES, SSD_D_INNER)),
            full((q, CONV_K)),
        ],
        out_specs=pl.BlockSpec((rows, SSD_D_INNER), lambda c: (c, 0)),
        out_shape=jax.ShapeDtypeStruct((t, SSD_D_INNER), BF16),
        scratch_shapes=[
            pltpu.VMEM((SSD_N_GROUPS, SSD_D_STATE, SSD_GROUP_WIDTH), F32),
            pltpu.VMEM((CONV_TAIL, SSD_CONV_DIM), BF16),
        ],
        compiler_params=_params("arbitrary"),
        name="ssd_core",
    )(zx, zx, zx, dt, conv_w_t, conv_b, pad_heads(dt_bias), pad_heads(a_log), d_exp, norm_w,
      expand, jnp.tile(expand, (3, 1)), _conv_shift_matrix())


def _ssd_mixer(h, mix_norm3, layer, j, ssd_w_in, ssd_conv_w, ssd_conv_b, ssd_dt_bias, ssd_a_log,
               ssd_d, ssd_norm, ssd_w_out):
    zx, dt = _ssd_in(h, mix_norm3, jnp.swapaxes(ssd_w_in, 1, 2), layer, j)
    head_of_channel = jnp.arange(SSD_D_INNER) // SSD_HEAD_DIM
    expand = (jnp.arange(LANES)[:, None] == head_of_channel[None, :]).astype(BF16)
    d_exp = jnp.repeat(ssd_d[j], SSD_HEAD_DIM).reshape(1, SSD_D_INNER)
    yn = _ssd_core(zx, dt, ssd_conv_w[j].T, ssd_conv_b[j].reshape(1, -1), ssd_dt_bias[j],
                   ssd_a_log[j], d_exp, ssd_norm[j].reshape(1, -1), expand)
    return _mm_res(yn, ssd_w_out, j, h)


def _gla_in_kernel(h_hbm, g_ref, w_ref, wl_ref, o_ref, gl_ref, n_ref, hbuf_ref, sem):
    @pl.when(pl.program_id(1) == 0)
    def _():
        n = _normed_rows(h_hbm, g_ref, n_ref, hbuf_ref, sem)
        gl_ref[...] = _dot_nt(n, _narrow_rows(wl_ref, GLA_GATE_RANK))

    o_ref[...] = _dot_nt(n_ref[...], w_ref[...].astype(BF16)).astype(o_ref.dtype)


def _gla_in(h, mix_norm3, gla_w_in_t, layer, j):
    t = h.shape[0]
    return pl.pallas_call(
        _gla_in_kernel,
        grid=(t // PROJ_TM, GLA_MAIN_DIM // PROJ_TN),
        in_specs=[
            pl.BlockSpec(memory_space=pl.ANY),
            pl.BlockSpec((None, 1, D_MODEL), lambda i, c: (layer, 0, 0)),
            pl.BlockSpec((None, PROJ_TN, D_MODEL), lambda i, c: (j, c, 0)),
            pl.BlockSpec((None, LANES, D_MODEL), lambda i, c: (j, GLA_MAIN_DIM // LANES, 0)),
        ],
        out_specs=[
            pl.BlockSpec((PROJ_TM, PROJ_TN), lambda i, c: (i, c)),
            pl.BlockSpec((PROJ_TM, LANES), lambda i, c: (i, 0)),
        ],
        out_shape=[
            jax.ShapeDtypeStruct((t, GLA_MAIN_DIM), BF16),
            jax.ShapeDtypeStruct((t, LANES), F32),
        ],
        scratch_shapes=PROJ_SCRATCH,
        compiler_params=_params("arbitrary", "arbitrary"),
        name="gla_in",
    )(h, mix_norm3, gla_w_in_t, gla_w_in_t)


def _gla_core_kernel(x_ref, gl_ref, w2_ref, bg_ref, nw_ref, o_ref, st_ref):
    q = GLA_CHUNK
    hk, hv = GLA_HEAD_K, GLA_HEAD_V

    @pl.when(pl.program_id(0) == 0)
    def _():
        st_ref[...] = jnp.zeros_like(st_ref)

    row = lax.broadcasted_iota(jnp.int32, (q, q), 0)
    col = lax.broadcasted_iota(jnp.int32, (q, q), 1)
    tril = row >= col
    lower = jnp.where(tril, 1.0, 0.0).astype(BF16)
    w2 = w2_ref[...].astype(BF16)

    def cum_log_decay(r0):
        lg = _dot(gl_ref[r0:r0 + q, :].astype(BF16), w2) + bg_ref[...]
        log_a = (jnp.minimum(lg, 0.0) - jnp.log1p(jnp.exp(-jnp.abs(lg)))) * (LOG2E / GLA_GATE_TAU)
        return sum(_dot(lower, p) for p in _split3(log_a))

    for pair in range(GLA_CHUNKS_PER_STEP // 2):
        ra = 2 * pair * q
        rb = ra + q
        bcum_a = cum_log_decay(ra)
        bcum_b = cum_log_decay(rb)
        for hd in range(GLA_N_HEADS):
            k0 = hd * hk
            v0 = 2 * GLA_D_K + hd * hv

            def chunk(r0, bcum):
                qh = x_ref[r0:r0 + q, k0:k0 + hk].astype(F32) * (hk ** -0.5)
                kh = x_ref[r0:r0 + q, GLA_D_K + k0:GLA_D_K + k0 + hk].astype(F32)
                vh = x_ref[r0:r0 + q, v0:v0 + hv]
                bc = bcum[:, k0:k0 + hk]
                mid = bc[q // 2:q // 2 + 1, :]
                end = bc[q - 1:q, :]
                att = _dot_nt((qh * jnp.exp2(bc - mid)).astype(BF16), (kh * jnp.exp2(mid - bc)).astype(BF16))
                att = jnp.where(tril, att, 0.0)
                o_intra = _dot(att.astype(BF16), vh)
                return o_intra, qh * jnp.exp2(bc), kh * jnp.exp2(end - bc), vh, jnp.exp2(end)

            oa, qin_a, kout_a, va, dec_a = chunk(ra, bcum_a)
            ob, qin_b, kout_b, vb, dec_b = chunk(rb, bcum_b)
            st = st_ref[hd]
            q2 = jnp.concatenate([qin_a, qin_b * dec_a], axis=0).astype(BF16)
            inter = _dot_nt(q2, st.astype(BF16))
            cross = _dot_nt(qin_b.astype(BF16), kout_a.astype(BF16))
            oa = oa + inter[0:q]
            ob = ob + inter[q:2 * q] + _dot(cross.astype(BF16), va)
            k2 = jnp.concatenate([kout_a * dec_b, kout_b], axis=0).astype(BF16)
            v2 = jnp.concatenate([va, vb], axis=0)
            st_ref[hd] = st * (dec_a * dec_b) + _dot_tn(v2, k2)
            for r0, o in ((ra, oa), (rb, ob)):
                rh = x_ref[r0:r0 + q, v0 + GLA_D_V:v0 + GLA_D_V + hv].astype(F32)
                o = _rms(o, nw_ref[...]) * _silu(rh)
                o_ref[r0:r0 + q, hd * hv:(hd + 1) * hv] = o.astype(o_ref.dtype)


def _gla_core(qkvr, g_low, w_gate2, b_gate, norm_w):
    t = qkvr.shape[0]
    rows = GLA_CHUNK * GLA_CHUNKS_PER_STEP
    return pl.pallas_call(
        _gla_core_kernel,
        grid=(t // rows,),
        in_specs=[
            pl.BlockSpec((rows, GLA_MAIN_DIM), lambda c: (c, 0)),
            pl.BlockSpec((rows, LANES), lambda c: (c, 0)),
            pl.BlockSpec((LANES, GLA_D_K), lambda c: (0, 0)),
            pl.BlockSpec((1, GLA_D_K), lambda c: (0, 0)),
            pl.BlockSpec((1, GLA_HEAD_V), lambda c: (0, 0)),
        ],
        out_specs=pl.BlockSpec((rows, GLA_D_V), lambda c: (c, 0)),
        out_shape=jax.ShapeDtypeStruct((t, GLA_D_V), BF16),
        scratch_shapes=[pltpu.VMEM((GLA_N_HEADS, GLA_HEAD_V, GLA_HEAD_K), F32)],
        compiler_params=_params("arbitrary"),
        name="gla_core",
    )(qkvr, g_low, w_gate2, b_gate, norm_w)


def _gla_mixer(h, mix_norm3, layer, j, gla_w_in, gla_w_gate2, gla_b_gate, gla_norm, gla_w_out):
    qkvr, g_low = _gla_in(h, mix_norm3, jnp.swapaxes(gla_w_in, 1, 2), layer, j)
    w_gate2 = jnp.pad(gla_w_gate2[j], ((0, LANES - GLA_GATE_RANK), (0, 0)))
    o = _gla_core(qkvr, g_low, w_gate2, gla_b_gate[j].reshape(1, -1), gla_norm[j].reshape(1, -1))
    return _mm_res(o, gla_w_out, j, h)


def _sgu_in_kernel(h_hbm, g_ref, w_ref, b_ref, o_ref, n_ref, hbuf_ref, sem):
    @pl.when(pl.program_id(1) == 0)
    def _():
        _normed_rows(h_hbm, g_ref, n_ref, hbuf_ref, sem)

    y = _dot(n_ref[...], w_ref[...].astype(BF16)) + b_ref[...]
    o_ref[...] = (0.5 * y * (1.0 + lax.erf(y * (0.5 ** 0.5)))).astype(o_ref.dtype)


def _sgu_in(h, mix_norm3, sgu_w_in, sgu_b_in3, layer, j):
    t = h.shape[0]
    return pl.pallas_call(
        _sgu_in_kernel,
        grid=(t // PROJ_TM, 2 * SGU_WIDTH // PROJ_TN),
        in_specs=[
            pl.BlockSpec(memory_space=pl.ANY),
            pl.BlockSpec((None, 1, D_MODEL), lambda i, c: (layer, 0, 0)),
            pl.BlockSpec((None, D_MODEL, PROJ_TN), lambda i, c: (j, 0, c)),
            pl.BlockSpec((None, 1, PROJ_TN), lambda i, c: (j, 0, c)),
        ],
        out_specs=pl.BlockSpec((PROJ_TM, PROJ_TN), lambda i, c: (i, c)),
        out_shape=jax.ShapeDtypeStruct((t, 2 * SGU_WIDTH), BF16),
        scratch_shapes=PROJ_SCRATCH,
        compiler_params=_params("arbitrary", "arbitrary"),
        name="sgu_in",
    )(h, mix_norm3, sgu_w_in, sgu_b_in3)


def _sgu_core_kernel(u_ref, v_ref, nw_ref, ws_ref, bs_ref, o_ref):
    q = SGU_CHUNK
    gd = SGU_GROUP_DIM
    row = lax.broadcasted_iota(jnp.int32, (q, q), 0)
    col = lax.broadcasted_iota(jnp.int32, (q, q), 1)
    tril = row >= col
    for ci in range(SGU_CHUNKS_PER_STEP):
        r0 = ci * q
        vn = _rms(v_ref[r0:r0 + q, :].astype(F32), nw_ref[...]).astype(BF16)
        for g in range(SGU_N_GROUPS):
            c0 = g * gd
            wc = jnp.where(tril, ws_ref[g], 0.0).astype(BF16)
            sv = _dot(wc, vn[:, c0:c0 + gd]) + bs_ref[:, g:g + 1]
            o_ref[r0:r0 + q, c0:c0 + gd] = (u_ref[r0:r0 + q, c0:c0 + gd].astype(F32) * sv).astype(o_ref.dtype)


def _sgu_core(zz, norm_w, w_s, b_s_t):
    t = zz.shape[0]
    rows = SGU_CHUNK * SGU_CHUNKS_PER_STEP
    return pl.pallas_call(
        _sgu_core_kernel,
        grid=(t // rows,),
        in_specs=[
            pl.BlockSpec((rows, SGU_WIDTH), lambda c: (c, 0)),
            pl.BlockSpec((rows, SGU_WIDTH), lambda c: (c, 1)),
            pl.BlockSpec((1, SGU_WIDTH), lambda c: (0, 0)),
            pl.BlockSpec((SGU_N_GROUPS, SGU_CHUNK, SGU_CHUNK), lambda c: (0, 0, 0)),
            pl.BlockSpec((SGU_CHUNK, SGU_N_GROUPS), lambda c: (0, 0)),
        ],
        out_specs=pl.BlockSpec((rows, SGU_WIDTH), lambda c: (c, 0)),
        out_shape=jax.ShapeDtypeStruct((t, SGU_WIDTH), BF16),
        compiler_params=_params("parallel"),
        name="sgu_core",
    )(zz, zz, norm_w, w_s, b_s_t)


def _sgu_mixer(h, mix_norm3, layer, j, sgu_w_in, sgu_b_in, sgu_norm, sgu_w_s, sgu_b_s, sgu_w_out):
    zz = _sgu_in(h, mix_norm3, sgu_w_in, sgu_b_in.reshape(sgu_b_in.shape[0], 1, -1), layer, j)
    y = _sgu_core(zz, sgu_norm[j].reshape(1, -1), sgu_w_s[j], sgu_b_s[j].T)
    return _mm_res(y, sgu_w_out, j, h)


def kernel(x, ffn_norm, ffn_w_in, ffn_w_out, mix_norm, ssd_w_in, ssd_conv_w, ssd_conv_b, ssd_dt_bias,
           ssd_a_log, ssd_d, ssd_norm, ssd_w_out, gla_w_in, gla_w_gate2, gla_b_gate, gla_norm, gla_w_out,
           sgu_w_in, sgu_b_in, sgu_norm, sgu_w_s, sgu_b_s, sgu_w_out, final_norm):
    b, t, d = x.shape
    assert (b, t, d) == (1, SEQ, D_MODEL)
    h = x.reshape(t, d)
    ffn_norm4 = ffn_norm.reshape(DEPTH, 2, 1, D_MODEL)
    mix_norm3 = mix_norm.reshape(DEPTH, 1, D_MODEL)
    final_w = final_norm.reshape(1, D_MODEL)
    for i in range(DEPTH):
        h = _ffn(h, ffn_norm4, ffn_w_in, ffn_w_out, final_w, i, 0, False)
        kind, j = i % N_MIXERS, i // N_MIXERS
        if kind == 0:
            h = _ssd_mixer(h, mix_norm3, i, j, ssd_w_in, ssd_conv_w, ssd_conv_b, ssd_dt_bias, ssd_a_log,
                           ssd_d, ssd_norm, ssd_w_out)
        elif kind == 1:
            h = _gla_mixer(h, mix_norm3, i, j, gla_w_in, gla_w_gate2, gla_b_gate, gla_norm, gla_w_out)
        else:
            h = _sgu_mixer(h, mix_norm3, i, j, sgu_w_in, sgu_b_in, sgu_norm, sgu_w_s, sgu_b_s, sgu_w_out)
        h = _ffn(h, ffn_norm4, ffn_w_in, ffn_w_out, final_w, i, 1, i == DEPTH - 1)
    return h.reshape(b, t, d)
```
